```python
import jax, jax.numpy as jnp
from jax import lax
import numpy as np

D_MODEL = 1024
BATCH = 4
SEQ = 8192
DEPTH = 1
DEC_BATCH = 128
DEC_SEQ = 8
PAST_LEN = 16384
PAGE_SIZE = 128

N_HEADS = 8
QK_NOPE = 64
QK_ROPE = 32
V_HEAD = 64
Q_LORA = 384
KV_LORA = 256
D_ATTN = N_HEADS * V_HEAD
D_CONV = D_MODEL // 2
CONV_WIDTH = 31
D_MIX = D_ATTN + D_CONV
D_IN = Q_LORA + KV_LORA + QK_ROPE + 2 * D_CONV
N_MEM = 256
X_HEADS = 4
X_HEAD_DIM = D_MODEL // X_HEADS
D_FF = 2816
FFN_CONV_WIDTH = 3

ROPE_THETA = 10000.0
EPS = 1e-6
Q_BLOCK = 128
NEG_INF = -1e30
ATTN_SCALE = (QK_NOPE + QK_ROPE) ** -0.5
X_SCALE = X_HEAD_DIM ** -0.5

kernel_name = 'hymba_mla_conformer_memxattn_convglu_step'


def rms_norm(x, g):
    xf = x.astype(jnp.float32)
    y = xf * lax.rsqrt(jnp.mean(xf * xf, axis=-1, keepdims=True) + EPS)
    return (y * g.astype(jnp.float32)).astype(x.dtype)


def layer_norm(x, g, b):
    xf = x.astype(jnp.float32)
    xc = xf - jnp.mean(xf, axis=-1, keepdims=True)
    var = jnp.mean(xc * xc, axis=-1, keepdims=True)
    return (xc * lax.rsqrt(var + EPS) * g.astype(jnp.float32) + b.astype(jnp.float32)).astype(x.dtype)


def rope(x, pos):
    half = x.shape[-1] // 2
    inv = ROPE_THETA ** (-jnp.arange(half, dtype=jnp.float32) / half)
    ang = pos.astype(jnp.float32)[:, None] * inv[None, :]
    ang = ang.reshape(ang.shape[:1] + (1,) * (x.ndim - 3) + (half,))
    cos, sin = jnp.cos(ang), jnp.sin(ang)
    xf = x.astype(jnp.float32)
    x1, x2 = xf[..., :half], xf[..., half:]
    return jnp.concatenate([x1 * cos - x2 * sin, x2 * cos + x1 * sin], axis=-1).astype(x.dtype)


def causal_dwconv(u, buf, w, b):
    full = jnp.concatenate([buf, u], axis=1)
    y = lax.conv_general_dilated(full, w[:, None, :], window_strides=(1,), padding='VALID',
                                 dimension_numbers=('NWC', 'WIO', 'NWC'),
                                 feature_group_count=u.shape[-1]) + b
    return y, full[:, full.shape[1] - buf.shape[1]:]


def mla_attend(q_lat, q_pe, c_keys, pe_keys, q_pos):
    B, S, H, L = q_lat.shape
    T = c_keys.shape[1]
    qb = Q_BLOCK if S % Q_BLOCK == 0 else S
    nb = S // qb
    k_pos = jnp.arange(T, dtype=jnp.int32)

    def block(args):
        ql, qp, pos = args
        s = (jnp.einsum('bqhl,btl->bhqt', ql, c_keys, preferred_element_type=jnp.float32)
             + jnp.einsum('bqhr,btr->bhqt', qp, pe_keys, preferred_element_type=jnp.float32)) * ATTN_SCALE
        s = jnp.where(pos[:, None] >= k_pos[None, :], s, NEG_INF)
        pr = jax.nn.softmax(s, axis=-1).astype(c_keys.dtype)
        return jnp.einsum('bhqt,btl->bqhl', pr, c_keys)

    to_blocks = lambda a: jnp.swapaxes(a.reshape((B, nb, qb) + a.shape[2:]), 0, 1)
    out = lax.map(block, (to_blocks(q_lat), to_blocks(q_pe), q_pos.reshape(nb, qb)))
    return jnp.swapaxes(out, 0, 1).reshape(B, S, H, L)


def token_mixer(h, pos, past_c, past_pe, conv_buf, p):
    B, S, _ = h.shape
    proj = h @ p['w_in']
    cuts = [Q_LORA, Q_LORA + KV_LORA, Q_LORA + KV_LORA + QK_ROPE, Q_LORA + KV_LORA + QK_ROPE + D_CONV]
    q_a, c_raw, pe_raw, glu_a, glu_b = jnp.split(proj, cuts, axis=-1)
    q = (rms_norm(q_a, p['g_q_a']) @ p['w_q_b']).reshape(B, S, N_HEADS, QK_NOPE + QK_ROPE)
    q_nope, q_pe = q[..., :QK_NOPE], rope(q[..., QK_NOPE:], pos)
    c_new = rms_norm(c_raw, p['g_kv_a'])
    pe_new = rope(pe_raw, pos)
    q_lat = jnp.einsum('bshn,lhn->bshl', q_nope, p['w_uk'])
    if past_c is None:
        c_keys, pe_keys = c_new, pe_new
    else:
        c_keys = jnp.concatenate([past_c, c_new], axis=1)
        pe_keys = jnp.concatenate([past_pe, pe_new], axis=1)
    o_lat = mla_attend(q_lat, q_pe, c_keys, pe_keys, pos)
    attn = jnp.einsum('bshl,lhv->bshv', o_lat, p['w_uv']).reshape(B, S, D_ATTN)
    u = glu_a * jax.nn.sigmoid(glu_b)
    cv, new_buf = causal_dwconv(u, conv_buf, p['w_dw'], p['b_dw'])
    cv = jax.nn.silu(layer_norm(cv, p['g_cln'], p['b_cln']))
    out = jnp.concatenate([attn, cv], axis=-1) @ p['w_out']
    return out, c_new, pe_new, new_buf


def mem_kv(mem, p):
    B, N, _ = mem.shape
    m = rms_norm(mem, p['g_mem'])
    k = (m @ p['w_xk']).reshape(B, N, X_HEADS, X_HEAD_DIM)
    v = (m @ p['w_xv']).reshape(B, N, X_HEADS, X_HEAD_DIM)
    return k, v


def cross_attend(h, mem_k, mem_v, p):
    B, S, _ = h.shape
    q = (h @ p['w_xq']).reshape(B, S, X_HEADS, X_HEAD_DIM)
    s = jnp.einsum('bshd,bnhd->bhsn', q, mem_k, preferred_element_type=jnp.float32) * X_SCALE
    pr = jax.nn.softmax(s, axis=-1).astype(mem_v.dtype)
    o = jnp.einsum('bhsn,bnhd->bshd', pr, mem_v).reshape(B, S, X_HEADS * X_HEAD_DIM)
    return o @ p['w_xo']


def conv_glu_ffn(h, buf, p):
    val, gate = jnp.split(h @ p['w_up'], 2, axis=-1)
    gate, new_buf = causal_dwconv(gate, buf, p['w_fdw'], p['b_fdw'])
    return (jax.nn.silu(gate) * val) @ p['w_down'], new_buf


def layer(x, pos, past_c, past_pe, conv_buf, ffn_buf, mem_k, mem_v, p):
    m, c_new, pe_new, conv_new = token_mixer(rms_norm(x, p['g_pre_mix']), pos, past_c, past_pe, conv_buf, p)
    x = x + rms_norm(m, p['g_post_mix'])
    xa = cross_attend(rms_norm(x, p['g_pre_x']), mem_k, mem_v, p)
    x = x + rms_norm(xa, p['g_post_x'])
    f, ffn_new = conv_glu_ffn(rms_norm(x, p['g_pre_ffn']), ffn_buf, p)
    x = x + rms_norm(f, p['g_post_ffn'])
    return x, c_new, pe_new, conv_new, ffn_new


def setup_inputs(seed: int = 0) -> dict:
    key = jax.random.key(seed)
    keys = iter(jax.random.split(key, 48))
    nk = lambda: next(keys)
    normal = lambda shape, scale: jax.random.normal(nk(), shape, jnp.float32) * scale
    w = lambda shape, fan_in: normal((DEPTH,) + shape, fan_in ** -0.5)
    gain = lambda n: 1.0 + normal((DEPTH, n), 0.05)
    bias = lambda n: normal((DEPTH, n), 0.01)

    n_pages = PAST_LEN // PAGE_SIZE
    n_used = DEC_BATCH * n_pages
    n_phys = n_used + n_used // 4
    perm = jax.random.permutation(nk(), n_phys)
    page_table = perm[:n_used].reshape(DEC_BATCH, n_pages).astype(jnp.int32)

    return {
        'x_prompt': normal((BATCH, SEQ, D_MODEL), 1.0),
        'x_sample': normal((DEC_BATCH, DEC_SEQ, D_MODEL), 1.0),
        'cache_kv_latent': normal((DEPTH, n_phys, PAGE_SIZE, KV_LORA), 1.0),
        'cache_k_rope': normal((DEPTH, n_phys, PAGE_SIZE, QK_ROPE), 1.0),
        'cache_mem_k': normal((DEPTH, DEC_BATCH, N_MEM, X_HEADS, X_HEAD_DIM), 1.0),
        'cache_mem_v': normal((DEPTH, DEC_BATCH, N_MEM, X_HEADS, X_HEAD_DIM), 1.0),
        'state_conv': normal((DEPTH, DEC_BATCH, CONV_WIDTH - 1, D_CONV), 0.5),
        'state_ffn_conv': normal((DEPTH, DEC_BATCH, FFN_CONV_WIDTH - 1, D_FF), 1.0),
        'page_table': page_table,
        'mem_prompt': normal((BATCH, N_MEM, D_MODEL), 1.0),
        'w_in': w((D_MODEL, D_IN), D_MODEL),
        'g_q_a': gain(Q_LORA),
        'w_q_b': w((Q_LORA, N_HEADS * (QK_NOPE + QK_ROPE)), Q_LORA),
        'g_kv_a': gain(KV_LORA),
        'w_uk': w((KV_LORA, N_HEADS, QK_NOPE), KV_LORA),
        'w_uv': w((KV_LORA, N_HEADS, V_HEAD), KV_LORA),
        'w_dw': w((CONV_WIDTH, D_CONV), CONV_WIDTH),
        'b_dw': bias(D_CONV),
        'g_cln': gain(D_CONV),
        'b_cln': bias(D_CONV),
        'w_out': w((D_MIX, D_MODEL), D_MIX),
        'g_mem': gain(D_MODEL),
        'w_xq': w((D_MODEL, X_HEADS * X_HEAD_DIM), D_MODEL),
        'w_xk': w((D_MODEL, X_HEADS * X_HEAD_DIM), D_MODEL),
        'w_xv': w((D_MODEL, X_HEADS * X_HEAD_DIM), D_MODEL),
        'w_xo': w((X_HEADS * X_HEAD_DIM, D_MODEL), X_HEADS * X_HEAD_DIM),
        'w_up': w((D_MODEL, 2 * D_FF), D_MODEL),
        'w_fdw': w((FFN_CONV_WIDTH, D_FF), FFN_CONV_WIDTH),
        'b_fdw': bias(D_FF),
        'w_down': w((D_FF, D_MODEL), D_FF),
        'g_pre_mix': gain(D_MODEL),
        'g_post_mix': gain(D_MODEL),
        'g_pre_x': gain(D_MODEL),
        'g_post_x': gain(D_MODEL),
        'g_pre_ffn': gain(D_MODEL),
        'g_post_ffn': gain(D_MODEL),
    }


def reference(x_prompt, x_sample, cache_kv_latent, cache_k_rope, cache_mem_k, cache_mem_v,
              state_conv, state_ffn_conv, page_table, mem_prompt,
              w_in, g_q_a, w_q_b, g_kv_a, w_uk, w_uv, w_dw, b_dw, g_cln, b_cln, w_out,
              g_mem, w_xq, w_xk, w_xv, w_xo, w_up, w_fdw, b_fdw, w_down,
              g_pre_mix, g_post_mix, g_pre_x, g_post_x, g_pre_ffn, g_post_ffn):
    B, S = x_prompt.shape[0], x_prompt.shape[1]
    DB, DS = x_sample.shape[0], x_sample.shape[1]
    n_past = page_table.shape[1] * PAGE_SIZE
    pos_p = jnp.arange(S, dtype=jnp.int32)
    pos_s = PAST_LEN + jnp.arange(DS, dtype=jnp.int32)
    yp, ys = x_prompt, x_sample
    c_p_l, pe_p_l, cb_p_l, fb_p_l, mk_p_l, mv_p_l = [], [], [], [], [], []
    c_s_l, pe_s_l, cb_s_l, fb_s_l = [], [], [], []
    for l in range(DEPTH):
        p = {
            'w_in': w_in[l], 'g_q_a': g_q_a[l], 'w_q_b': w_q_b[l], 'g_kv_a': g_kv_a[l],
            'w_uk': w_uk[l], 'w_uv': w_uv[l], 'w_dw': w_dw[l], 'b_dw': b_dw[l],
            'g_cln': g_cln[l], 'b_cln': b_cln[l], 'w_out': w_out[l],
            'g_mem': g_mem[l], 'w_xq': w_xq[l], 'w_xk': w_xk[l], 'w_xv': w_xv[l], 'w_xo': w_xo[l],
            'w_up': w_up[l], 'w_fdw': w_fdw[l], 'b_fdw': b_fdw[l], 'w_down': w_down[l],
            'g_pre_mix': g_pre_mix[l], 'g_post_mix': g_post_mix[l],
            'g_pre_x': g_pre_x[l], 'g_post_x': g_post_x[l],
            'g_pre_ffn': g_pre_ffn[l], 'g_post_ffn': g_post_ffn[l],
        }
        mk_p, mv_p = mem_kv(mem_prompt, p)
        yp, c_p, pe_p, cb_p, fb_p = layer(
            yp, pos_p, None, None,
            jnp.zeros((B, CONV_WIDTH - 1, D_CONV), yp.dtype),
            jnp.zeros((B, FFN_CONV_WIDTH - 1, D_FF), yp.dtype),
            mk_p, mv_p, p)
        past_c = cache_kv_latent[l][page_table].reshape(DB, n_past, KV_LORA)
        past_pe = cache_k_rope[l][page_table].reshape(DB, n_past, QK_ROPE)
        ys, c_s, pe_s, cb_s, fb_s = layer(
            ys, pos_s, past_c, past_pe, state_conv[l], state_ffn_conv[l],
            cache_mem_k[l], cache_mem_v[l], p)
        c_p_l.append(c_p); pe_p_l.append(pe_p); cb_p_l.append(cb_p); fb_p_l.append(fb_p)
        mk_p_l.append(mk_p); mv_p_l.append(mv_p)
        c_s_l.append(c_s); pe_s_l.append(pe_s); cb_s_l.append(cb_s); fb_s_l.append(fb_s)
    kv_latent_prompt = jnp.stack(c_p_l)
    k_rope_prompt = jnp.stack(pe_p_l)
    conv_state_prompt = jnp.stack(cb_p_l)
    ffn_state_prompt = jnp.stack(fb_p_l)
    mem_k_prompt = jnp.stack(mk_p_l)
    mem_v_prompt = jnp.stack(mv_p_l)
    kv_latent_sample = jnp.stack(c_s_l)
    k_rope_sample = jnp.stack(pe_s_l)
    conv_state_sample = jnp.stack(cb_s_l)
    ffn_state_sample = jnp.stack(fb_s_l)
    return (yp, ys, kv_latent_prompt, k_rope_prompt, conv_state_prompt, ffn_state_prompt,
            mem_k_prompt, mem_v_prompt, kv_latent_sample, k_rope_sample, conv_state_sample,
            ffn_state_sample)
```

```python
import functools
import math

import numpy as np
import jax
import jax.numpy as jnp
from jax import lax
from jax.experimental import pallas as pl
from jax.experimental.pallas import tpu as pltpu

F32 = jnp.float32
BF16 = jnp.bfloat16

PAGE_SIZE = 128
N_HEADS = 8
QK_NOPE = 64
QK_ROPE = 32
V_HEAD = 64
Q_LORA = 384
KV_LORA = 256
D_CONV = 512
CONV_WIDTH = 31
X_HEADS = 4
X_HEAD_DIM = 256
D_FF = 2816
FFN_CONV_WIDTH = 3
ROPE_THETA = 10000.0
EPS = 1e-6
NEG_INF = -1e30
ATTN_SCALE = (QK_NOPE + QK_ROPE) ** -0.5
X_SCALE = X_HEAD_DIM ** -0.5

LANES = 128
SUBLANES = 8
VMEM_LIMIT = 56 * 1024 * 1024

HEAD_BLOCK = LANES
ROPE_LANE0 = QK_NOPE
ROPE_HALF = QK_ROPE // 2
CONV_HALO = 32
FFN_HALO = SUBLANES
ROPE_TABLE_ROWS = 256

_C_Q = 0
_C_C = _C_Q + Q_LORA
_C_PE = _C_C + KV_LORA
_C_GA = _C_PE + HEAD_BLOCK
_C_GB = _C_GA + D_CONV
D_IN_PAD = _C_GB + D_CONV


def _params(*sem):
    return pltpu.CompilerParams(dimension_semantics=sem, vmem_limit_bytes=VMEM_LIMIT)


def _full(shape):
    n = len(shape)
    return pl.BlockSpec(shape, lambda *_: (0,) * n)


def _rms(x, g):
    return x * lax.rsqrt(jnp.mean(x * x, axis=-1, keepdims=True) + EPS) * g


def _dot(a, b):
    return jnp.dot(a, b, preferred_element_type=F32)


def _dot_nt(a, b):
    return lax.dot_general(a, b, (((1,), (1,)), ((), ())), preferred_element_type=F32)


def _rope_table_kernel(inv_ref, cos_ref, sin_a_ref, sin_b_ref, *, pos0, period):
    n = cos_ref.shape[0]
    row = pl.program_id(0) * n + lax.broadcasted_iota(jnp.int32, (n, LANES), 0)
    lane = lax.broadcasted_iota(jnp.int32, (n, LANES), 1)
    pos = (pos0 + row % period).astype(F32)
    ang = pos * inv_ref[...]
    cos, sin = jnp.cos(ang), jnp.sin(ang)
    first = (lane >= ROPE_LANE0) & (lane < ROPE_LANE0 + ROPE_HALF)
    second = (lane >= ROPE_LANE0 + ROPE_HALF) & (lane < ROPE_LANE0 + QK_ROPE)
    cos_ref[...] = jnp.where(first | second, cos, 1.0)
    sin_a_ref[...] = jnp.where(first, -sin, 0.0)
    sin_b_ref[...] = jnp.where(second, sin, 0.0)


def _rope_tables(n_rows, pos0, period):
    inv = np.zeros((1, LANES), np.float32)
    freq = ROPE_THETA ** (-np.arange(ROPE_HALF, dtype=np.float64) / ROPE_HALF)
    inv[0, ROPE_LANE0:ROPE_LANE0 + ROPE_HALF] = freq
    inv[0, ROPE_LANE0 + ROPE_HALF:ROPE_LANE0 + QK_ROPE] = freq
    out = jax.ShapeDtypeStruct((n_rows, LANES), F32)
    rows = min(n_rows, ROPE_TABLE_ROWS)
    blk = pl.BlockSpec((rows, LANES), lambda i: (i, 0))
    return pl.pallas_call(
        functools.partial(_rope_table_kernel, pos0=pos0, period=period),
        grid=(n_rows // rows,),
        in_specs=[_full(inv.shape)],
        out_specs=(blk, blk, blk),
        out_shape=(out, out, out),
        compiler_params=_params("parallel"),
        name="rope_tables",
    )(jnp.asarray(inv))


def _rope_block(x, cos, sin_a, sin_b):
    left = pltpu.roll(x, LANES - ROPE_HALF, 1)
    right = pltpu.roll(x, ROPE_HALF, 1)
    return x * cos + left * sin_a + right * sin_b


def _mix_in_kernel(x_ref, cos_ref, sin_a_ref, sin_b_ref, g_pre_ref, w_in_ref, g_q_ref, w_q_ref,
                   g_kv_ref, w_a_ref, w_b_ref, *out_refs, absorb):
    x = x_ref[0]
    h = _rms(x, g_pre_ref[...]).astype(BF16)
    proj = _dot(h, w_in_ref[...])
    cos, sin_a, sin_b = cos_ref[...], sin_a_ref[...], sin_b_ref[...]

    qn = _rms(proj[:, _C_Q:_C_C], g_q_ref[...]).astype(BF16)
    q = _dot(qn, w_q_ref[...])
    c_new = _rms(proj[:, _C_C:_C_PE], g_kv_ref[...])
    pe_blk = _rope_block(proj[:, _C_PE:_C_GA], cos, sin_a, sin_b)
    u = proj[:, _C_GA:_C_GB] * jax.nn.sigmoid(proj[:, _C_GB:])

    if absorb:
        q_ref, qlat_ref, c_ref, pe_ref, u_ref = out_refs
    else:
        q_ref, k_ref, v_ref, c_ref, pe_ref, u_ref = out_refs
        cb = c_new.astype(BF16)
        k = _dot(cb, w_a_ref[...])
        v_ref[0] = _dot(cb, w_b_ref[...]).astype(BF16)

    for hd in range(N_HEADS):
        blk = slice(hd * HEAD_BLOCK, (hd + 1) * HEAD_BLOCK)
        q_h = (_rope_block(q[:, blk], cos, sin_a, sin_b) * ATTN_SCALE).astype(BF16)
        q_ref[0, :, blk] = q_h
        if absorb:
            qlat_ref[0, :, hd * KV_LORA:(hd + 1) * KV_LORA] = _dot(
                q_h[:, :QK_NOPE], w_a_ref[hd]).astype(BF16)
        else:
            k_ref[0, :, blk] = (k[:, blk] + pe_blk).astype(BF16)

    c_ref[0] = c_new
    pe_ref[0] = pe_blk[:, ROPE_LANE0:ROPE_LANE0 + QK_ROPE]
    u_ref[0] = u


def _mix_in(x, tables, table_map, wts, *, tm, absorb):
    B, S, D = x.shape
    grid = (B, S // tm)
    row = lambda b, i: (b, i, 0)
    tab = pl.BlockSpec((tm, LANES), table_map)
    w_a, w_b = (wts["w_uk_t"], wts["w_uv_pad"]) if absorb else (wts["w_k"], wts["w_v"])
    ins = [x, *tables, wts["g_pre_mix"], wts["w_in"], wts["g_q_a"], wts["w_q"], wts["g_kv_a"], w_a, w_b]
    in_specs = [pl.BlockSpec((1, tm, D), row), tab, tab, tab] + [_full(a.shape) for a in ins[4:]]
    hq = N_HEADS * HEAD_BLOCK
    out = [((B, S, hq), BF16)]
    if absorb:
        out += [((B, S, N_HEADS * KV_LORA), BF16)]
    else:
        out += [((B, S, hq), BF16), ((B, S, N_HEADS * V_HEAD), BF16)]
    out += [((B, S, KV_LORA), F32), ((B, S, QK_ROPE), F32), ((B, S, D_CONV), F32)]
    return pl.pallas_call(
        functools.partial(_mix_in_kernel, absorb=absorb),
        grid=grid,
        in_specs=in_specs,
        out_specs=[pl.BlockSpec((1, tm, s[-1]), row) for s, _ in out],
        out_shape=[jax.ShapeDtypeStruct(s, d) for s, d in out],
        compiler_params=_params("parallel", "arbitrary"),
        name="mix_in_sample" if absorb else "mix_in_prompt",
    )(*ins)


def _flash_kernel(q_ref, k_ref, v_ref, o_ref, *, tq):
    qi = pl.program_id(2)
    row = lax.broadcasted_iota(jnp.int32, (tq, tq), 0)
    col = lax.broadcasted_iota(jnp.int32, (tq, tq), 1)
    lane = lax.broadcasted_iota(jnp.int32, (tq, LANES), 1)
    outs = []
    for j in range(2):
        blk = slice(j * HEAD_BLOCK, (j + 1) * HEAD_BLOCK)
        q = q_ref[0, :, blk]

        def tile(t, carry, masked):
            m, l, acc = carry
            start = pl.multiple_of(t * tq, tq)
            k = k_ref[0, pl.ds(start, tq), blk]
            v = v_ref[0, pl.ds(start, tq), :]
            s = _dot_nt(q, k)
            if masked:
                s = jnp.where(row >= col, s, NEG_INF)
            m_new = jnp.maximum(m, jnp.max(s, axis=-1, keepdims=True))
            alpha = jnp.exp(m - m_new)
            p = jnp.exp(s - m_new)
            l = alpha * l + jnp.sum(p, axis=-1, keepdims=True)
            acc = alpha * acc + _dot(p.astype(BF16), v)
            return m_new, l, acc

        init = (jnp.full((tq, 1), NEG_INF, F32), jnp.zeros((tq, 1), F32), jnp.zeros((tq, LANES), F32))
        carry = lax.fori_loop(0, qi, functools.partial(tile, masked=False), init)
        m, l, acc = tile(qi, carry, True)
        outs.append(acc / l)
    o_ref[0] = jnp.where(lane < V_HEAD, outs[0], outs[1]).astype(BF16)


def _flash(q, k, v, *, tq):
    B, S, _ = q.shape
    pairs = N_HEADS // 2
    return pl.pallas_call(
        functools.partial(_flash_kernel, tq=tq),
        grid=(B, pairs, S // tq),
        in_specs=[
            pl.BlockSpec((1, tq, 2 * HEAD_BLOCK), lambda b, p, i: (b, i, p)),
            pl.BlockSpec((1, S, 2 * HEAD_BLOCK), lambda b, p, i: (b, 0, p)),
            pl.BlockSpec((1, S, 2 * V_HEAD), lambda b, p, i: (b, 0, p)),
        ],
        out_specs=pl.BlockSpec((1, tq, 2 * V_HEAD), lambda b, p, i: (b, i, p)),
        out_shape=jax.ShapeDtypeStruct((B, S, N_HEADS * V_HEAD), BF16),
        compiler_params=_params("parallel", "parallel", "arbitrary"),
        name="flash_prompt",
    )(q, k, v)


def _paged_kernel(pt_ref, qlat_ref, qpe_ref, cnew_ref, penew_ref, wuv_ref, ckv_hbm, kr_hbm, o_ref,
                  cbuf, pbuf, sem, m_scr, l_scr, acc_scr, *, pages_per_chunk, sub_keys):
    b, c = pl.program_id(0), pl.program_id(1)
    nb, nc = pl.num_programs(0), pl.num_programs(1)
    step = b * nc + c
    slot = step % 2
    chunk_keys = pages_per_chunk * PAGE_SIZE
    n_rows = qlat_ref.shape[1]

    def page_copies(bb, cc, sl, j):
        page = pt_ref[bb, cc * pages_per_chunk + j]
        rows = pl.ds(j * PAGE_SIZE, PAGE_SIZE)
        return (pltpu.make_async_copy(ckv_hbm.at[page], cbuf.at[sl, rows], sem.at[sl]),
                pltpu.make_async_copy(kr_hbm.at[page], pbuf.at[sl, rows], sem.at[sl]))

    def start_chunk(bb, cc, sl):
        for j in range(pages_per_chunk):
            for cp in page_copies(bb, cc, sl, j):
                cp.start()

    @pl.when(step == 0)
    def _():
        start_chunk(b, c, slot)

    @pl.when(step + 1 < nb * nc)
    def _():
        nxt = step + 1
        start_chunk(nxt // nc, nxt % nc, 1 - slot)

    for j in range(pages_per_chunk):
        for cp in page_copies(b, c, slot, j):
            cp.wait()

    @pl.when(c == 0)
    def _():
        m_scr[...] = jnp.full(m_scr.shape, NEG_INF, F32)
        l_scr[...] = jnp.zeros(l_scr.shape, F32)
        acc_scr[...] = jnp.zeros(acc_scr.shape, F32)

    qlat = qlat_ref[0]
    qpe = qpe_ref[0]

    def update(carry, s, vals):
        m, l, acc = carry
        m_new = jnp.maximum(m, jnp.max(s, axis=-1, keepdims=True))
        alpha = jnp.exp(m - m_new)
        p = jnp.exp(s - m_new)
        l = alpha * l + jnp.sum(p, axis=-1, keepdims=True)
        acc = alpha * acc + _dot(p.astype(BF16), vals)
        return m_new, l, acc

    def sub_block(t, carry):
        rows = pl.ds(pl.multiple_of(t * sub_keys, sub_keys), sub_keys)
        cb = cbuf[slot, rows, :].astype(BF16)
        pb = pbuf[slot, rows, :].astype(BF16)
        s = _dot_nt(qlat, cb) + _dot_nt(qpe, pb)
        return update(carry, s, cb)

    carry = lax.fori_loop(0, chunk_keys // sub_keys, sub_block, (m_scr[...], l_scr[...], acc_scr[...]))
    m_scr[...], l_scr[...], acc_scr[...] = carry

    @pl.when(c == nc - 1)
    def _():
        n_new = cnew_ref.shape[1]
        cb = cnew_ref[0].astype(BF16)
        pb = penew_ref[0].astype(BF16)
        s = _dot_nt(qlat, cb) + _dot_nt(qpe, pb)
        q_s = lax.broadcasted_iota(jnp.int32, (n_rows, n_new), 0) % n_new
        t = lax.broadcasted_iota(jnp.int32, (n_rows, n_new), 1)
        s = jnp.where(q_s >= t, s, NEG_INF)
        _, l, acc = update(carry, s, cb)
        o_lat = acc / l
        out = jnp.zeros((n_new, N_HEADS * V_HEAD), F32)
        for hd in range(N_HEADS):
            out = out + _dot(o_lat[hd * n_new:(hd + 1) * n_new].astype(BF16), wuv_ref[hd])
        o_ref[0] = out


def _paged_attention(page_table, qlat, qpe, c_new, pe_new, w_uv_pad, cache_c, cache_pe,
                     *, pages_per_chunk, sub_keys):
    DB, n_rows, _ = qlat.shape
    n_new = c_new.shape[1]
    n_pages = page_table.shape[1]
    nc = n_pages // pages_per_chunk
    chunk_keys = pages_per_chunk * PAGE_SIZE
    per_b = lambda b, c, pt: (b, 0, 0)
    grid_spec = pltpu.PrefetchScalarGridSpec(
        num_scalar_prefetch=1,
        grid=(DB, nc),
        in_specs=[
            pl.BlockSpec((1, n_rows, KV_LORA), per_b),
            pl.BlockSpec((1, n_rows, QK_ROPE), per_b),
            pl.BlockSpec((1, n_new, KV_LORA), per_b),
            pl.BlockSpec((1, n_new, QK_ROPE), per_b),
            pl.BlockSpec(w_uv_pad.shape, lambda b, c, pt: (0, 0, 0)),
            pl.BlockSpec(memory_space=pl.ANY),
            pl.BlockSpec(memory_space=pl.ANY),
        ],
        out_specs=pl.BlockSpec((1, n_new, N_HEADS * V_HEAD), per_b),
        scratch_shapes=[
            pltpu.VMEM((2, chunk_keys, KV_LORA), F32),
            pltpu.VMEM((2, chunk_keys, QK_ROPE), F32),
            pltpu.SemaphoreType.DMA((2,)),
            pltpu.VMEM((n_rows, 1), F32),
            pltpu.VMEM((n_rows, 1), F32),
            pltpu.VMEM((n_rows, KV_LORA), F32),
        ],
    )
    return pl.pallas_call(
        functools.partial(_paged_kernel, pages_per_chunk=pages_per_chunk, sub_keys=sub_keys),
        grid_spec=grid_spec,
        out_shape=jax.ShapeDtypeStruct((DB, n_new, N_HEADS * V_HEAD), F32),
        compiler_params=_params("arbitrary", "arbitrary"),
        name="paged_sample",
    )(page_table, qlat, qpe, c_new, pe_new, w_uv_pad, cache_c, cache_pe)


def _mem_kv_kernel(mem_ref, g_ref, w_k_ref, w_v_ref, k_ref, v_ref):
    m = _rms(mem_ref[0], g_ref[...]).astype(BF16)
    k_ref[0] = _dot(m, w_k_ref[...])
    v_ref[0] = _dot(m, w_v_ref[...])


def _mem_kv(mem, g_mem, w_xk, w_xv):
    B, N, D = mem.shape
    blk = pl.BlockSpec((1, N, D), lambda b: (b, 0, 0))
    out = jax.ShapeDtypeStruct((B, N, X_HEADS * X_HEAD_DIM), F32)
    return pl.pallas_call(
        _mem_kv_kernel,
        grid=(B,),
        in_specs=[blk, _full(g_mem.shape), _full(w_xk.shape), _full(w_xv.shape)],
        out_specs=[blk, blk],
        out_shape=[out, out],
        compiler_params=_params("parallel"),
        name="mem_kv",
    )(mem, g_mem, w_xk, w_xv)


def _post_mix_kernel(x_ref, attn_ref, u_ref, halo_ref, mk_ref, mv_ref, w_dw_ref, b_dw_ref, g_cln_ref,
                     b_cln_ref, w_out_ref, g_post_mix_ref, g_pre_x_ref, w_xq_ref, w_xo_ref,
                     g_post_x_ref, x2_ref, cstate_ref, full_scr, *, fresh_sequence, conv_rows):
    G, T, D = x_ref.shape
    i = pl.program_id(1)
    halo = halo_ref[...]
    if fresh_sequence:
        halo = jnp.where(i == 0, 0.0, halo)
    full_scr[:, :CONV_HALO, :] = halo
    full_scr[:, CONV_HALO:, :] = u_ref[...]
    cstate_ref[...] = full_scr[:, T + CONV_HALO - (CONV_WIDTH - 1):, :]

    base = CONV_HALO - (CONV_WIDTH - 1)
    cv_parts = []
    for g in range(G):
        for r0 in range(0, T, conv_rows):
            acc = jnp.broadcast_to(b_dw_ref[...], (conv_rows, D_CONV))
            for k in range(CONV_WIDTH):
                acc = acc + w_dw_ref[k:k + 1, :] * full_scr[g, base + r0 + k:base + r0 + k + conv_rows, :]
            cv_parts.append(acc)
    cv = jnp.concatenate(cv_parts, axis=0) if len(cv_parts) > 1 else cv_parts[0]

    xc = cv - jnp.mean(cv, axis=-1, keepdims=True)
    var = jnp.mean(xc * xc, axis=-1, keepdims=True)
    cv = xc * lax.rsqrt(var + EPS) * g_cln_ref[...] + b_cln_ref[...]
    cv = (cv * jax.nn.sigmoid(cv)).astype(BF16)

    attn = attn_ref[...].reshape(G * T, attn_ref.shape[2]).astype(BF16)
    d_attn = attn.shape[1]
    mix = _dot(attn, w_out_ref[:d_attn, :]) + _dot(cv, w_out_ref[d_attn:, :])
    x = x_ref[...].reshape(G * T, D)
    x1 = x + _rms(mix, g_post_mix_ref[...])

    hx = _rms(x1, g_pre_x_ref[...]).astype(BF16)
    q = _dot(hx, w_xq_ref[...]) * X_SCALE
    o_rows = []
    for g in range(G):
        o_heads = []
        for hd in range(X_HEADS):
            cols = slice(hd * X_HEAD_DIM, (hd + 1) * X_HEAD_DIM)
            mk = mk_ref[g, :, cols].astype(BF16)
            mv = mv_ref[g, :, cols].astype(BF16)
            s = _dot_nt(q[g * T:(g + 1) * T, cols].astype(BF16), mk)
            p = jnp.exp(s - jnp.max(s, axis=-1, keepdims=True))
            l = jnp.sum(p, axis=-1, keepdims=True)
            o_heads.append(_dot(p.astype(BF16), mv) / l)
        o_rows.append(jnp.concatenate(o_heads, axis=-1))
    o = jnp.concatenate(o_rows, axis=0) if G > 1 else o_rows[0]
    xa = _dot(o.astype(BF16), w_xo_ref[...])
    x2 = x1 + _rms(xa, g_post_x_ref[...])
    x2_ref[...] = x2.reshape(G, T, D)


def _post_mix(x, attn, u, halo, halo_map, mk, mv, wts, *, G, T, fresh_sequence, conv_rows):
    B, S, D = x.shape
    grid = (B // G, S // T)
    row = lambda b, i: (b, i, 0)
    per_b = lambda b, i: (b, 0, 0)
    names = ["w_dw", "b_dw", "g_cln", "b_cln", "w_out", "g_post_mix", "g_pre_x", "w_xq", "w_xo", "g_post_x"]
    w = [wts[n] for n in names]
    n_state = CONV_WIDTH - 1
    return pl.pallas_call(
        functools.partial(_post_mix_kernel, fresh_sequence=fresh_sequence, conv_rows=conv_rows),
        grid=grid,
        in_specs=[
            pl.BlockSpec((G, T, D), row),
            pl.BlockSpec((G, T, attn.shape[2]), row),
            pl.BlockSpec((G, T, D_CONV), row),
            pl.BlockSpec((G, CONV_HALO, D_CONV), halo_map),
            pl.BlockSpec((G,) + mk.shape[1:], per_b),
            pl.BlockSpec((G,) + mv.shape[1:], per_b),
        ] + [_full(a.shape) for a in w],
        out_specs=[pl.BlockSpec((G, T, D), row), pl.BlockSpec((G, n_state, D_CONV), per_b)],
        out_shape=[jax.ShapeDtypeStruct((B, S, D), F32), jax.ShapeDtypeStruct((B, n_state, D_CONV), F32)],
        scratch_shapes=[pltpu.VMEM((G, T + CONV_HALO, D_CONV), F32)],
        compiler_params=_params("parallel", "arbitrary"),
        name="post_mix_sample" if G > 1 else "post_mix_prompt",
    )(x, attn, u, halo, mk, mv, *w)


def _ffn_kernel(x_ref, state_ref, g_pre_ref, w_up_ref, w_fdw_ref, b_fdw_ref, w_down_ref, g_post_ref,
                y_ref, fstate_ref, gate_scr):
    G, T, D = x_ref.shape
    i = pl.program_id(1)
    n_hist = FFN_CONV_WIDTH - 1
    hist = slice(FFN_HALO - n_hist, FFN_HALO)

    @pl.when(i == 0)
    def _():
        gate_scr[:, hist, :] = state_ref[...]

    x = x_ref[...].reshape(G * T, D)
    h = _rms(x, g_pre_ref[...]).astype(BF16)
    up = _dot(h, w_up_ref[...])
    val = up[:, :D_FF]
    gate_scr[:, FFN_HALO:, :] = up[:, D_FF:].reshape(G, T, D_FF)
    gate = jnp.broadcast_to(b_fdw_ref[...], (G, T, D_FF))
    for k in range(FFN_CONV_WIDTH):
        off = FFN_HALO - n_hist + k
        gate = gate + w_fdw_ref[k:k + 1, :] * gate_scr[:, off:off + T, :]
    new_state = gate_scr[:, T + FFN_HALO - n_hist:, :]
    fstate_ref[...] = new_state
    gate_scr[:, hist, :] = new_state

    gate = gate.reshape(G * T, D_FF)
    act = (gate * jax.nn.sigmoid(gate) * val).astype(BF16)
    f = _dot(act, w_down_ref[...])
    y_ref[...] = (x + _rms(f, g_post_ref[...])).reshape(G, T, D)


def _ffn(x, state, wts, *, G, T):
    B, S, D = x.shape
    row = lambda b, i: (b, i, 0)
    per_b = lambda b, i: (b, 0, 0)
    names = ["g_pre_ffn", "w_up", "w_fdw", "b_fdw", "w_down", "g_post_ffn"]
    w = [wts[n] for n in names]
    n_hist = FFN_CONV_WIDTH - 1
    return pl.pallas_call(
        _ffn_kernel,
        grid=(B // G, S // T),
        in_specs=[pl.BlockSpec((G, T, D), row), pl.BlockSpec((G, n_hist, D_FF), per_b)]
        + [pl.BlockSpec(a.shape, lambda b, i, n=a.ndim: (0,) * n, pipeline_mode=pl.Buffered(1)) for a in w],
        out_specs=[pl.BlockSpec((G, T, D), row), pl.BlockSpec((G, n_hist, D_FF), per_b)],
        out_shape=[jax.ShapeDtypeStruct((B, S, D), F32), jax.ShapeDtypeStruct((B, n_hist, D_FF), F32)],
        scratch_shapes=[pltpu.VMEM((G, T + FFN_HALO, D_FF), F32)],
        compiler_params=_params("parallel", "arbitrary"),
        name="ffn_sample" if G > 1 else "ffn_prompt",
    )(x, state, *w)


def _prepare_weights(w_in, g_q_a, w_q_b, g_kv_a, w_uk, w_uv, w_dw, b_dw, g_cln, b_cln, w_out,
                     w_xq, w_xo, w_up, w_fdw, b_fdw, w_down,
                     g_pre_mix, g_post_mix, g_pre_x, g_post_x, g_pre_ffn, g_post_ffn):
    assert w_in.shape[0] == 1, "one trunk layer"
    d_model = w_in.shape[1]
    zpad = lambda a, n: jnp.zeros(a.shape[:-1] + (n,), a.dtype)
    w_in0 = w_in[0]
    cuts = [Q_LORA, Q_LORA + KV_LORA, Q_LORA + KV_LORA + QK_ROPE, Q_LORA + KV_LORA + QK_ROPE + D_CONV]
    q_a, c_raw, pe_raw, glu_a, glu_b = jnp.split(w_in0, cuts, axis=-1)
    pe_blk = jnp.concatenate([zpad(pe_raw, ROPE_LANE0), pe_raw,
                              zpad(pe_raw, HEAD_BLOCK - ROPE_LANE0 - QK_ROPE)], axis=-1)
    w_in_p = jnp.concatenate([q_a, c_raw, pe_blk, glu_a, glu_b], axis=-1)
    assert w_in_p.shape == (d_model, D_IN_PAD)

    qk = QK_NOPE + QK_ROPE
    w_q = w_q_b[0].reshape(Q_LORA, N_HEADS, qk)
    w_q = jnp.concatenate([w_q, zpad(w_q, HEAD_BLOCK - qk)], axis=-1).reshape(Q_LORA, N_HEADS * HEAD_BLOCK)
    w_k = jnp.concatenate([w_uk[0], zpad(w_uk[0], HEAD_BLOCK - QK_NOPE)], axis=-1)
    w_k = w_k.reshape(KV_LORA, N_HEADS * HEAD_BLOCK)
    w_v = w_uv[0].reshape(KV_LORA, N_HEADS * V_HEAD)
    w_uk_t = jnp.transpose(w_uk[0], (1, 2, 0))
    eye = jnp.eye(N_HEADS, dtype=w_uv.dtype)
    w_uv_pad = jnp.einsum("lhv,hg->hlgv", w_uv[0], eye).reshape(N_HEADS, KV_LORA, N_HEADS * V_HEAD)

    bf = lambda a: a.astype(BF16)
    return {
        "w_in": bf(w_in_p), "g_q_a": g_q_a, "w_q": bf(w_q), "g_kv_a": g_kv_a,
        "w_k": bf(w_k), "w_v": bf(w_v), "w_uk_t": bf(w_uk_t), "w_uv_pad": bf(w_uv_pad),
        "w_dw": w_dw[0], "b_dw": b_dw, "g_cln": g_cln, "b_cln": b_cln, "w_out": bf(w_out[0]),
        "w_xq": bf(w_xq[0]), "w_xo": bf(w_xo[0]), "w_up": bf(w_up[0]), "w_fdw": w_fdw[0],
        "b_fdw": b_fdw, "w_down": bf(w_down[0]),
        "g_pre_mix": g_pre_mix, "g_post_mix": g_post_mix, "g_pre_x": g_pre_x, "g_post_x": g_post_x,
        "g_pre_ffn": g_pre_ffn, "g_post_ffn": g_post_ffn,
    }


TM_MIX = 512
TQ_FLASH = 512
T_POST = 512
T_FFN = 256
CONV_ROWS = 64
SAMPLE_SEQS = 16
SAMPLE_SEQS_POST = 4
PAGES_PER_CHUNK = 64
SUB_KEYS = 1024


def kernel(x_prompt, x_sample, cache_kv_latent, cache_k_rope, cache_mem_k, cache_mem_v, state_conv, state_ffn_conv, page_table, mem_prompt, w_in, g_q_a, w_q_b, g_kv_a, w_uk, w_uv, w_dw, b_dw, g_cln, b_cln, w_out, g_mem, w_xq, w_xk, w_xv, w_xo, w_up, w_fdw, b_fdw, w_down, g_pre_mix, g_post_mix, g_pre_x, g_post_x, g_pre_ffn, g_post_ffn):
    B, S, D = x_prompt.shape
    DB, DS, _ = x_sample.shape
    past_len = page_table.shape[1] * PAGE_SIZE
    wts = _prepare_weights(w_in, g_q_a, w_q_b, g_kv_a, w_uk, w_uv, w_dw, b_dw, g_cln, b_cln, w_out,
                           w_xq, w_xo, w_up, w_fdw, b_fdw, w_down,
                           g_pre_mix, g_post_mix, g_pre_x, g_post_x, g_pre_ffn, g_post_ffn)

    mk_p, mv_p = _mem_kv(mem_prompt, g_mem, w_xk[0].astype(BF16), w_xv[0].astype(BF16))
    tables_p = _rope_tables(S, 0, S)
    q, k, v, c_p, pe_p, u_p = _mix_in(x_prompt, tables_p, lambda b, i: (i, 0), wts, tm=TM_MIX, absorb=False)
    attn_p = _flash(q, k, v, tq=TQ_FLASH)
    halo_blocks = T_POST // CONV_HALO
    x2_p, cstate_p = _post_mix(
        x_prompt, attn_p, u_p, u_p, lambda b, i: (b, jnp.maximum(i * halo_blocks - 1, 0), 0),
        mk_p, mv_p, wts, G=1, T=T_POST, fresh_sequence=True, conv_rows=CONV_ROWS)
    y_p, fstate_p = _ffn(x2_p, jnp.zeros((B, FFN_CONV_WIDTH - 1, D_FF), F32), wts, G=1, T=T_FFN)

    rows = SAMPLE_SEQS * DS
    tables_s = _rope_tables(rows, past_len, DS)
    q_s, qlat_s, c_s, pe_s, u_s = _mix_in(
        x_sample.reshape(1, DB * DS, D), tables_s, lambda b, i: (0, 0), wts, tm=rows, absorb=True)
    qlat_s = qlat_s.reshape(DB, DS, N_HEADS, KV_LORA).transpose(0, 2, 1, 3).reshape(DB, N_HEADS * DS, KV_LORA)
    qpe_s = q_s.reshape(DB, DS, N_HEADS, HEAD_BLOCK)[..., ROPE_LANE0:ROPE_LANE0 + QK_ROPE]
    qpe_s = qpe_s.transpose(0, 2, 1, 3).reshape(DB, N_HEADS * DS, QK_ROPE)
    c_s = c_s.reshape(DB, DS, KV_LORA)
    pe_s = pe_s.reshape(DB, DS, QK_ROPE)
    attn_s = _paged_attention(page_table, qlat_s, qpe_s, c_s, pe_s, wts["w_uv_pad"],
                              cache_kv_latent.reshape(cache_kv_latent.shape[1:]),
                              cache_k_rope.reshape(cache_k_rope.shape[1:]),
                              pages_per_chunk=PAGES_PER_CHUNK, sub_keys=SUB_KEYS)
    n_state = CONV_WIDTH - 1
    halo_s = jnp.pad(state_conv[0], ((0, 0), (CONV_HALO - n_state, 0), (0, 0)))
    mem_cols = X_HEADS * X_HEAD_DIM
    x2_s, cstate_s = _post_mix(
        x_sample, attn_s, u_s.reshape(DB, DS, D_CONV), halo_s, lambda b, i: (b, 0, 0),
        cache_mem_k[0].reshape(DB, -1, mem_cols), cache_mem_v[0].reshape(DB, -1, mem_cols), wts,
        G=SAMPLE_SEQS_POST, T=DS, fresh_sequence=False, conv_rows=DS)
    y_s, fstate_s = _ffn(x2_s, state_ffn_conv[0], wts, G=SAMPLE_SEQS, T=DS)

    mem_shape = (1, B, mem_prompt.shape[1], X_HEADS, X_HEAD_DIM)
    return (y_p, y_s, c_p[None], pe_p[None], cstate_p[None], fstate_p[None],
            mk_p.reshape(mem_shape), mv_p.reshape(mem_shape),
            c_s[None], pe_s[None], cstate_s[None], fstate_s[None])
```

```python
import functools
import math

import numpy as np
import jax
import jax.numpy as jnp
from jax import lax
from jax.experimental import pallas as pl
from jax.experimental.pallas import tpu as pltpu

F32 = jnp.float32
BF16 = jnp.bfloat16

PAGE_SIZE = 128
N_HEADS = 8
QK_NOPE = 64
QK_ROPE = 32
V_HEAD = 64
Q_LORA = 384
KV_LORA = 256
D_CONV = 512
CONV_WIDTH = 31
X_HEADS = 4
X_HEAD_DIM = 256
D_FF = 2816
FFN_CONV_WIDTH = 3
ROPE_THETA = 10000.0
EPS = 1e-6
NEG_INF = -1e30
LOG2E = math.log2(math.e)
ATTN_SCALE = (QK_NOPE + QK_ROPE) ** -0.5 * LOG2E
X_SCALE = X_HEAD_DIM ** -0.5 * LOG2E

LANES = 128
SUBLANES = 8
VMEM_LIMIT = 56 * 1024 * 1024

HEAD_BLOCK = LANES
ROPE_LANE0 = QK_NOPE
ROPE_HALF = QK_ROPE // 2
CONV_HALO = 32
FFN_HALO = SUBLANES
ROPE_TABLE_ROWS = 256
N_SLOTS = 2

_C_Q = 0
_C_C = _C_Q + Q_LORA
_C_PE = _C_C + KV_LORA
_C_GA = _C_PE + HEAD_BLOCK
_C_GB = _C_GA + D_CONV
D_IN_PAD = _C_GB + D_CONV


def _params(*sem):
    return pltpu.CompilerParams(dimension_semantics=sem, vmem_limit_bytes=VMEM_LIMIT)


def _full(shape):
    n = len(shape)
    return pl.BlockSpec(shape, lambda *_: (0,) * n)


def _rms(x, g):
    return x * lax.rsqrt(jnp.mean(x * x, axis=-1, keepdims=True) + EPS) * g


def _dot(a, b):
    return jnp.dot(a, b, preferred_element_type=F32)


def _dot_nt(a, b):
    return lax.dot_general(a, b, (((1,), (1,)), ((), ())), preferred_element_type=F32)


def _rope_table_kernel(inv_ref, cos_ref, sin_a_ref, sin_b_ref, *, pos0, period):
    n = cos_ref.shape[0]
    row = pl.program_id(0) * n + lax.broadcasted_iota(jnp.int32, (n, LANES), 0)
    lane = lax.broadcasted_iota(jnp.int32, (n, LANES), 1)
    pos = (pos0 + row % period).astype(F32)
    ang = pos * inv_ref[...]
    cos, sin = jnp.cos(ang), jnp.sin(ang)
    first = (lane >= ROPE_LANE0) & (lane < ROPE_LANE0 + ROPE_HALF)
    second = (lane >= ROPE_LANE0 + ROPE_HALF) & (lane < ROPE_LANE0 + QK_ROPE)
    cos_ref[...] = jnp.where(first | second, cos, 1.0)
    sin_a_ref[...] = jnp.where(first, -sin, 0.0)
    sin_b_ref[...] = jnp.where(second, sin, 0.0)


def _rope_tables(n_rows, pos0, period):
    inv = np.zeros((1, LANES), np.float32)
    freq = ROPE_THETA ** (-np.arange(ROPE_HALF, dtype=np.float64) / ROPE_HALF)
    inv[0, ROPE_LANE0:ROPE_LANE0 + ROPE_HALF] = freq
    inv[0, ROPE_LANE0 + ROPE_HALF:ROPE_LANE0 + QK_ROPE] = freq
    out = jax.ShapeDtypeStruct((n_rows, LANES), F32)
    rows = min(n_rows, ROPE_TABLE_ROWS)
    blk = pl.BlockSpec((rows, LANES), lambda i: (i, 0))
    return pl.pallas_call(
        functools.partial(_rope_table_kernel, pos0=pos0, period=period),
        grid=(n_rows // rows,),
        in_specs=[_full(inv.shape)],
        out_specs=(blk, blk, blk),
        out_shape=(out, out, out),
        compiler_params=_params("parallel"),
        name="rope_tables",
    )(jnp.asarray(inv))


def _rope_block(x, cos, sin_a, sin_b):
    left = pltpu.roll(x, LANES - ROPE_HALF, 1)
    right = pltpu.roll(x, ROPE_HALF, 1)
    return x * cos + left * sin_a + right * sin_b


def _mix_in_kernel(x_ref, cos_ref, sin_a_ref, sin_b_ref, g_pre_ref, w_in_ref, g_q_ref, w_q_ref,
                   g_kv_ref, w_a_ref, w_b_ref, *out_refs, absorb):
    x = x_ref[0]
    h = _rms(x, g_pre_ref[...]).astype(BF16)
    proj = _dot(h, w_in_ref[...])
    cos, sin_a, sin_b = cos_ref[...], sin_a_ref[...], sin_b_ref[...]

    qn = _rms(proj[:, _C_Q:_C_C], g_q_ref[...]).astype(BF16)
    q = _dot(qn, w_q_ref[...])
    c_new = _rms(proj[:, _C_C:_C_PE], g_kv_ref[...])
    pe_blk = _rope_block(proj[:, _C_PE:_C_GA], cos, sin_a, sin_b)
    u = proj[:, _C_GA:_C_GB] * jax.nn.sigmoid(proj[:, _C_GB:])

    if absorb:
        q_ref, qlat_ref, c_ref, pe_ref, u_ref = out_refs
    else:
        q_ref, k_ref, v_ref, c_ref, pe_ref, u_ref = out_refs
        cb = c_new.astype(BF16)
        k = _dot(cb, w_a_ref[...])
        v_ref[0] = _dot(cb, w_b_ref[...]).astype(BF16)

    for hd in range(N_HEADS):
        blk = slice(hd * HEAD_BLOCK, (hd + 1) * HEAD_BLOCK)
        q_h = (_rope_block(q[:, blk], cos, sin_a, sin_b) * ATTN_SCALE).astype(BF16)
        q_ref[0, :, blk] = q_h
        if absorb:
            qlat_ref[0, :, hd * KV_LORA:(hd + 1) * KV_LORA] = _dot(
                q_h[:, :QK_NOPE], w_a_ref[hd]).astype(BF16)
        else:
            k_ref[0, :, blk] = (k[:, blk] + pe_blk).astype(BF16)

    c_ref[0] = c_new
    pe_ref[0] = pe_blk[:, ROPE_LANE0:ROPE_LANE0 + QK_ROPE]
    u_ref[0] = u


def _mix_in(x, tables, table_map, wts, *, tm, absorb):
    B, S, D = x.shape
    grid = (B, S // tm)
    row = lambda b, i: (b, i, 0)
    tab = pl.BlockSpec((tm, LANES), table_map)
    w_a, w_b = (wts["w_uk_t"], wts["w_uv_pad"]) if absorb else (wts["w_k"], wts["w_v"])
    ins = [x, *tables, wts["g_pre_mix"], wts["w_in"], wts["g_q_a"], wts["w_q"], wts["g_kv_a"], w_a, w_b]
    in_specs = [pl.BlockSpec((1, tm, D), row), tab, tab, tab] + [_full(a.shape) for a in ins[4:]]
    hq = N_HEADS * HEAD_BLOCK
    out = [((B, S, hq), BF16)]
    if absorb:
        out += [((B, S, N_HEADS * KV_LORA), BF16)]
    else:
        out += [((B, S, hq), BF16), ((B, S, N_HEADS * V_HEAD), BF16)]
    out += [((B, S, KV_LORA), F32), ((B, S, QK_ROPE), F32), ((B, S, D_CONV), F32)]
    return pl.pallas_call(
        functools.partial(_mix_in_kernel, absorb=absorb),
        grid=grid,
        in_specs=in_specs,
        out_specs=[pl.BlockSpec((1, tm, s[-1]), row) for s, _ in out],
        out_shape=[jax.ShapeDtypeStruct(s, d) for s, d in out],
        compiler_params=_params("parallel", "arbitrary"),
        name="mix_in_sample" if absorb else "mix_in_prompt",
    )(*ins)


def _transpose_bf16(x):
    return x.astype(F32).T.astype(BF16)


def _flash_kernel(q_ref, k_ref, v_ref, o_ref, vt_scr, s0_scr, m_scr, l_scr, acc_scr, *, tq):
    qi = pl.program_id(2)
    n_kv = vt_scr.shape[0]

    @pl.when(qi == 0)
    def _():
        def put(t, _):
            rows = pl.ds(pl.multiple_of(t * tq, tq), tq)
            vt_scr[t] = _transpose_bf16(v_ref[0, rows, :])
            return 0
        lax.fori_loop(0, n_kv, put, 0)

    key = lax.broadcasted_iota(jnp.int32, (tq, tq), 0)
    qry = lax.broadcasted_iota(jnp.int32, (tq, tq), 1)
    blks = [slice(j * HEAD_BLOCK, (j + 1) * HEAD_BLOCK) for j in range(2)]
    q_t = [_transpose_bf16(q_ref[0, :, blk]) for blk in blks]

    def scores(t, j):
        rows = pl.ds(pl.multiple_of(t * tq, tq), tq)
        return _dot(k_ref[0, rows, blks[j]], q_t[j])

    def absorb_tile(t, j, s, masked):
        if masked:
            s = jnp.where(key <= qry, s, NEG_INF)
        m = m_scr[j]
        m_new = jnp.maximum(m, jnp.max(s, axis=0, keepdims=True))
        alpha = jnp.exp2(m - m_new)
        p = jnp.exp2(s - m_new)
        m_scr[j] = m_new
        l_scr[j] = alpha * l_scr[j] + jnp.sum(p, axis=0, keepdims=True)
        acc_scr[j] = alpha * acc_scr[j] + _dot(vt_scr[t], p.astype(BF16))

    m_scr[...] = jnp.full(m_scr.shape, NEG_INF, F32)
    l_scr[...] = jnp.zeros(l_scr.shape, F32)
    acc_scr[...] = jnp.zeros(acc_scr.shape, F32)

    s0_scr[...] = scores(0, 0)

    def full_tile(t):
        s1 = scores(t, 1)
        absorb_tile(t, 0, s0_scr[...], False)
        s0_scr[...] = scores(t + 1, 0)
        absorb_tile(t, 1, s1, False)

    def two_tiles(tp, _):
        full_tile(2 * tp)
        full_tile(2 * tp + 1)
        return 0

    lax.fori_loop(0, qi // 2, two_tiles, 0)

    @pl.when(qi % 2 == 1)
    def _():
        full_tile(qi - 1)

    s1 = scores(qi, 1)
    absorb_tile(qi, 0, s0_scr[...], True)
    absorb_tile(qi, 1, s1, True)
    row = lax.broadcasted_iota(jnp.int32, (2 * V_HEAD, tq), 0)
    o_t = jnp.where(row < V_HEAD, acc_scr[0] / l_scr[0], acc_scr[1] / l_scr[1])
    o_ref[0] = o_t.T.astype(BF16)


def _flash(q, k, v, *, tq):
    B, S, _ = q.shape
    pairs = N_HEADS // 2
    return pl.pallas_call(
        functools.partial(_flash_kernel, tq=tq),
        grid=(B, pairs, S // tq),
        in_specs=[
            pl.BlockSpec((1, tq, 2 * HEAD_BLOCK), lambda b, p, i: (b, i, p)),
            pl.BlockSpec((1, S, 2 * HEAD_BLOCK), lambda b, p, i: (b, 0, p)),
            pl.BlockSpec((1, S, 2 * V_HEAD), lambda b, p, i: (b, 0, p)),
        ],
        out_specs=pl.BlockSpec((1, tq, 2 * V_HEAD), lambda b, p, i: (b, i, p)),
        out_shape=jax.ShapeDtypeStruct((B, S, N_HEADS * V_HEAD), BF16),
        scratch_shapes=[
            pltpu.VMEM((S // tq, 2 * V_HEAD, tq), BF16),
            pltpu.VMEM((tq, tq), F32),
            pltpu.VMEM((2, 1, tq), F32),
            pltpu.VMEM((2, 1, tq), F32),
            pltpu.VMEM((2, 2 * V_HEAD, tq), F32),
        ],
        compiler_params=_params("parallel", "parallel", "arbitrary"),
        name="flash_prompt",
    )(q, k, v)


def _paged_kernel(pt_ref, qlat_ref, qpe_ref, cnew_ref, penew_ref, wuv_ref, ckv_hbm, kr_hbm, o_ref,
                  cbuf, pbuf, sem, cb16, *, pages_per_chunk, sub_keys):
    b = pl.program_id(0)
    n_chunks = pl.num_programs(0) * N_SLOTS
    n_rows = qlat_ref.shape[1]
    chunk_keys = pages_per_chunk * PAGE_SIZE

    def page_copies(chunk, sl, j):
        page = pt_ref[chunk * pages_per_chunk + j]
        keys = pl.ds(j * PAGE_SIZE, PAGE_SIZE)
        return (pltpu.make_async_copy(ckv_hbm.at[page], cbuf.at[sl, keys], sem.at[sl]),
                pltpu.make_async_copy(kr_hbm.at[page], pbuf.at[sl, :, keys], sem.at[sl]))

    def start_chunk(chunk, sl):
        for j in range(pages_per_chunk):
            for cp in page_copies(chunk, sl, j):
                cp.start()

    def wait_chunk(chunk, sl):
        for j in range(pages_per_chunk):
            for cp in page_copies(chunk, sl, j):
                cp.wait()

    @pl.when(b == 0)
    def _():
        start_chunk(0, 0)

    qlat = qlat_ref[0]
    qpe = qpe_ref[0]

    def update(carry, s, vals):
        m, l, acc = carry
        m_new = jnp.maximum(m, jnp.max(s, axis=-1, keepdims=True))
        alpha = jnp.exp2(m - m_new)
        p = jnp.exp2(s - m_new)
        l = alpha * l + jnp.sum(p, axis=-1, keepdims=True)
        acc = alpha * acc + _dot(p.astype(BF16), vals)
        return m_new, l, acc

    n_sub = chunk_keys // sub_keys
    carry = (jnp.full((n_rows, 1), NEG_INF, F32), jnp.zeros((n_rows, 1), F32), jnp.zeros((n_rows, KV_LORA), F32))
    for sl in range(N_SLOTS):
        chunk = b * N_SLOTS + sl
        wait_chunk(chunk, sl)
        nxt = jnp.where(chunk + 1 == n_chunks, 0, chunk + 1)
        start_chunk(nxt, 1 - sl)

        def scores(i):
            keys = slice(i * sub_keys, (i + 1) * sub_keys)
            cb16[keys, :] = cbuf[sl, keys, :].astype(BF16)
            return _dot_nt(qlat, cb16[keys, :]) + _dot(qpe, pbuf[sl, :, keys].astype(BF16))

        s = scores(0)
        for i in range(n_sub):
            s_next = scores(i + 1) if i + 1 < n_sub else None
            carry = update(carry, s, cb16[i * sub_keys:(i + 1) * sub_keys, :])
            s = s_next

    n_new = cnew_ref.shape[1]
    cb = cnew_ref[0].astype(BF16)
    pb = penew_ref[0].astype(BF16)
    s = _dot_nt(qlat, cb) + _dot_nt(qpe, pb)
    q_s = lax.broadcasted_iota(jnp.int32, (n_rows, n_new), 0) % n_new
    t = lax.broadcasted_iota(jnp.int32, (n_rows, n_new), 1)
    s = jnp.where(q_s >= t, s, NEG_INF)
    _, l, acc = update(carry, s, cb)
    o_lat = acc / l
    out = jnp.zeros((n_new, N_HEADS * V_HEAD), F32)
    for hd in range(N_HEADS):
        out = out + _dot(o_lat[hd * n_new:(hd + 1) * n_new].astype(BF16), wuv_ref[hd])
    o_ref[0] = out

    @pl.when(b == pl.num_programs(0) - 1)
    def _():
        wait_chunk(0, 0)


def _paged_attention(page_table, qlat, qpe, c_new, pe_new, w_uv_pad, cache_c, cache_pe_t,
                     *, pages_per_chunk, sub_keys):
    DB, n_rows, _ = qlat.shape
    n_new = c_new.shape[1]
    n_pages = page_table.shape[1]
    assert n_pages == N_SLOTS * pages_per_chunk
    chunk_keys = pages_per_chunk * PAGE_SIZE
    per_b = lambda b, pt: (b, 0, 0)
    grid_spec = pltpu.PrefetchScalarGridSpec(
        num_scalar_prefetch=1,
        grid=(DB,),
        in_specs=[
            pl.BlockSpec((1, n_rows, KV_LORA), per_b),
            pl.BlockSpec((1, n_rows, QK_ROPE), per_b),
            pl.BlockSpec((1, n_new, KV_LORA), per_b),
            pl.BlockSpec((1, n_new, QK_ROPE), per_b),
            pl.BlockSpec(w_uv_pad.shape, lambda b, pt: (0, 0, 0)),
            pl.BlockSpec(memory_space=pl.ANY),
            pl.BlockSpec(memory_space=pl.ANY),
        ],
        out_specs=pl.BlockSpec((1, n_new, N_HEADS * V_HEAD), per_b),
        scratch_shapes=[
            pltpu.VMEM((N_SLOTS, chunk_keys, KV_LORA), F32),
            pltpu.VMEM((N_SLOTS, QK_ROPE, chunk_keys), F32),
            pltpu.SemaphoreType.DMA((N_SLOTS,)),
            pltpu.VMEM((chunk_keys, KV_LORA), BF16),
        ],
    )
    return pl.pallas_call(
        functools.partial(_paged_kernel, pages_per_chunk=pages_per_chunk, sub_keys=sub_keys),
        grid_spec=grid_spec,
        out_shape=jax.ShapeDtypeStruct((DB, n_new, N_HEADS * V_HEAD), F32),
        compiler_params=_params("arbitrary"),
        name="paged_sample",
    )(page_table.reshape(-1), qlat, qpe, c_new, pe_new, w_uv_pad, cache_c, cache_pe_t)


def _mem_kv_kernel(mem_ref, g_ref, w_k_ref, w_v_ref, k_ref, v_ref, kb_ref, vb_ref):
    m = _rms(mem_ref[0], g_ref[...]).astype(BF16)
    k = _dot(m, w_k_ref[...])
    v = _dot(m, w_v_ref[...])
    k_ref[0], v_ref[0] = k, v
    kb_ref[0], vb_ref[0] = k.astype(BF16), v.astype(BF16)


def _mem_kv(mem, g_mem, w_xk, w_xv):
    B, N, D = mem.shape
    blk = pl.BlockSpec((1, N, D), lambda b: (b, 0, 0))
    cols = X_HEADS * X_HEAD_DIM
    out_blk = pl.BlockSpec((1, N, cols), lambda b: (b, 0, 0))
    out = jax.ShapeDtypeStruct((B, N, cols), F32)
    out_b = jax.ShapeDtypeStruct((B, N, cols), BF16)
    return pl.pallas_call(
        _mem_kv_kernel,
        grid=(B,),
        in_specs=[blk, _full(g_mem.shape), _full(w_xk.shape), _full(w_xv.shape)],
        out_specs=[out_blk] * 4,
        out_shape=[out, out, out_b, out_b],
        compiler_params=_params("parallel"),
        name="mem_kv",
    )(mem, g_mem, w_xk, w_xv)


def _post_mix_kernel(x_ref, attn_ref, u_ref, halo_ref, mk_ref, mv_ref, w_dw_ref, b_dw_ref, g_cln_ref,
                     b_cln_ref, w_out_ref, g_post_mix_ref, g_pre_x_ref, w_xq_ref, w_xo_ref,
                     g_post_x_ref, x2_ref, cstate_ref, full_scr, shift_scr, *, fresh_sequence, conv_rows,
                     n_pieces):
    G, T, D = x_ref.shape
    i = pl.program_id(1)
    halo = halo_ref[...]
    if fresh_sequence:
        halo = jnp.where(i == 0, 0.0, halo)
    full_scr[:, :CONV_HALO, :] = halo
    full_scr[:, CONV_HALO:, :] = u_ref[...]
    cstate_ref[...] = full_scr[:, T + CONV_HALO - (CONV_WIDTH - 1):, :]
    n_shift = shift_scr.shape[2]
    for j in range(1, SUBLANES):
        shift_scr[j - 1] = full_scr[:, j:j + n_shift, :]

    if G == 1:
        pieces = [([0], p * (T // n_pieces), T // n_pieces) for p in range(n_pieces)]
    else:
        per = G // n_pieces
        pieces = [(list(range(p * per, (p + 1) * per)), 0, T) for p in range(n_pieces)]

    def rows_of(ref, piece, cast=None):
        gs, r0, nr = piece
        parts = [ref[g, r0:r0 + nr, :] for g in gs]
        out = jnp.concatenate(parts, axis=0) if len(parts) > 1 else parts[0]
        return out if cast is None else out.astype(cast)

    def conv_group(piece):
        gs, r0, nr = piece
        base = CONV_HALO - (CONV_WIDTH - 1)
        parts = []
        for g in gs:
            for c0 in range(r0, r0 + nr, conv_rows):
                acc = jnp.broadcast_to(b_dw_ref[...], (conv_rows, D_CONV))
                for k in range(CONV_WIDTH):
                    a, j = divmod(base + k, SUBLANES)
                    rows = slice(c0 + a * SUBLANES, c0 + a * SUBLANES + conv_rows)
                    src = full_scr[g, rows, :] if j == 0 else shift_scr[j - 1, g, rows, :]
                    acc = acc + w_dw_ref[k:k + 1, :] * src
                parts.append(acc)
        cv = jnp.concatenate(parts, axis=0) if len(parts) > 1 else parts[0]
        xc = cv - jnp.mean(cv, axis=-1, keepdims=True)
        var = jnp.mean(xc * xc, axis=-1, keepdims=True)
        cv = xc * lax.rsqrt(var + EPS) * g_cln_ref[...] + b_cln_ref[...]
        return (cv * jax.nn.sigmoid(cv)).astype(BF16)

    d_attn = attn_ref.shape[2]
    cvs = [conv_group(pc) for pc in pieces]
    mixes = [_dot(rows_of(attn_ref, pc, BF16), w_out_ref[:d_attn, :]) + _dot(cv, w_out_ref[d_attn:, :])
             for pc, cv in zip(pieces, cvs)]
    x1s = [rows_of(x_ref, pc) + _rms(mix, g_post_mix_ref[...]) for pc, mix in zip(pieces, mixes)]

    qs = [_dot(_rms(x1, g_pre_x_ref[...]).astype(BF16), w_xq_ref[...]) * X_SCALE for x1 in x1s]
    os_ = []
    for (gs, r0, nr), q in zip(pieces, qs):
        heads = [(gi, g, hd, slice(hd * X_HEAD_DIM, (hd + 1) * X_HEAD_DIM))
                 for gi, g in enumerate(gs) for hd in range(X_HEADS)]
        scores = [_dot_nt(q[gi * nr:(gi + 1) * nr, cols].astype(BF16), mk_ref[g, :, cols])
                  for gi, g, hd, cols in heads]
        outs = []
        for (gi, g, hd, cols), s in zip(heads, scores):
            p = jnp.exp2(s - jnp.max(s, axis=-1, keepdims=True))
            l = jnp.sum(p, axis=-1, keepdims=True)
            outs.append(_dot(p.astype(BF16), mv_ref[g, :, cols]) / l)
        o_rows = [jnp.concatenate(outs[gi * X_HEADS:(gi + 1) * X_HEADS], axis=-1) for gi in range(len(gs))]
        os_.append(jnp.concatenate(o_rows, axis=0) if len(o_rows) > 1 else o_rows[0])
    xas = [_dot(o.astype(BF16), w_xo_ref[...]) for o in os_]
    for (gs, r0, nr), x1, xa in zip(pieces, x1s, xas):
        x2 = x1 + _rms(xa, g_post_x_ref[...])
        for gi, g in enumerate(gs):
            x2_ref[g, r0:r0 + nr, :] = x2[gi * nr:(gi + 1) * nr]


def _post_mix(x, attn, u, halo, halo_map, mk, mv, wts, *, G, T, fresh_sequence, conv_rows, n_pieces):
    B, S, D = x.shape
    grid = (B // G, S // T)
    row = lambda b, i: (b, i, 0)
    per_b = lambda b, i: (b, 0, 0)
    names = ["w_dw", "b_dw", "g_cln", "b_cln", "w_out", "g_post_mix", "g_pre_x", "w_xq", "w_xo", "g_post_x"]
    w = [wts[n] for n in names]
    n_state = CONV_WIDTH - 1
    return pl.pallas_call(
        functools.partial(_post_mix_kernel, fresh_sequence=fresh_sequence, conv_rows=conv_rows,
                          n_pieces=n_pieces),
        grid=grid,
        in_specs=[
            pl.BlockSpec((G, T, D), row),
            pl.BlockSpec((G, T, attn.shape[2]), row),
            pl.BlockSpec((G, T, D_CONV), row),
            pl.BlockSpec((G, CONV_HALO, D_CONV), halo_map),
            pl.BlockSpec((G,) + mk.shape[1:], per_b),
            pl.BlockSpec((G,) + mv.shape[1:], per_b),
        ] + [_full(a.shape) for a in w],
        out_specs=[pl.BlockSpec((G, T, D), row), pl.BlockSpec((G, n_state, D_CONV), per_b)],
        out_shape=[jax.ShapeDtypeStruct((B, S, D), F32), jax.ShapeDtypeStruct((B, n_state, D_CONV), F32)],
        scratch_shapes=[pltpu.VMEM((G, T + CONV_HALO, D_CONV), F32),
                        pltpu.VMEM((SUBLANES - 1, G, T + CONV_HALO - SUBLANES, D_CONV), F32)],
        compiler_params=_params("parallel", "arbitrary"),
        name="post_mix_sample" if G > 1 else "post_mix_prompt",
    )(x, attn, u, halo, mk, mv, *w)


def _ffn_kernel(x_ref, state_ref, g_pre_ref, w_up_ref, w_fdw_ref, b_fdw_ref, w_down_ref, g_post_ref,
                y_ref, fstate_ref, gate_scr):
    G, T, D = x_ref.shape
    i = pl.program_id(1)
    n_hist = FFN_CONV_WIDTH - 1
    hist = slice(FFN_HALO - n_hist, FFN_HALO)

    @pl.when(i == 0)
    def _():
        gate_scr[:, hist, :] = state_ref[...]

    x = x_ref[...].reshape(G * T, D)
    h = _rms(x, g_pre_ref[...]).astype(BF16)
    up = _dot(h, w_up_ref[...])
    val = up[:, :D_FF]
    gate_scr[:, FFN_HALO:, :] = up[:, D_FF:].reshape(G, T, D_FF)
    gate = jnp.broadcast_to(b_fdw_ref[...], (G, T, D_FF))
    for k in range(FFN_CONV_WIDTH):
        off = FFN_HALO - n_hist + k
        gate = gate + w_fdw_ref[k:k + 1, :] * gate_scr[:, off:off + T, :]
    new_state = gate_scr[:, T + FFN_HALO - n_hist:, :]
    fstate_ref[...] = new_state
    gate_scr[:, hist, :] = new_state

    gate = gate.reshape(G * T, D_FF)
    act = (gate * jax.nn.sigmoid(gate) * val).astype(BF16)
    f = _dot(act, w_down_ref[...])
    y_ref[...] = (x + _rms(f, g_post_ref[...])).reshape(G, T, D)


def _ffn(x, state, wts, *, G, T):
    B, S, D = x.shape
    row = lambda b, i: (b, i, 0)
    per_b = lambda b, i: (b, 0, 0)
    names = ["g_pre_ffn", "w_up", "w_fdw", "b_fdw", "w_down", "g_post_ffn"]
    w = [wts[n] for n in names]
    n_hist = FFN_CONV_WIDTH - 1
    return pl.pallas_call(
        _ffn_kernel,
        grid=(B // G, S // T),
        in_specs=[pl.BlockSpec((G, T, D), row), pl.BlockSpec((G, n_hist, D_FF), per_b)]
        + [pl.BlockSpec(a.shape, lambda b, i, n=a.ndim: (0,) * n, pipeline_mode=pl.Buffered(1)) for a in w],
        out_specs=[pl.BlockSpec((G, T, D), row), pl.BlockSpec((G, n_hist, D_FF), per_b)],
        out_shape=[jax.ShapeDtypeStruct((B, S, D), F32), jax.ShapeDtypeStruct((B, n_hist, D_FF), F32)],
        scratch_shapes=[pltpu.VMEM((G, T + FFN_HALO, D_FF), F32)],
        compiler_params=_params("parallel", "arbitrary"),
        name="ffn_sample" if G > 1 else "ffn_prompt",
    )(x, state, *w)


def _prepare_weights(w_in, g_q_a, w_q_b, g_kv_a, w_uk, w_uv, w_dw, b_dw, g_cln, b_cln, w_out,
                     w_xq, w_xo, w_up, w_fdw, b_fdw, w_down,
                     g_pre_mix, g_post_mix, g_pre_x, g_post_x, g_pre_ffn, g_post_ffn):
    assert w_in.shape[0] == 1, "one trunk layer"
    d_model = w_in.shape[1]
    zpad = lambda a, n: jnp.zeros(a.shape[:-1] + (n,), a.dtype)
    w_in0 = w_in[0]
    cuts = [Q_LORA, Q_LORA + KV_LORA, Q_LORA + KV_LORA + QK_ROPE, Q_LORA + KV_LORA + QK_ROPE + D_CONV]
    q_a, c_raw, pe_raw, glu_a, glu_b = jnp.split(w_in0, cuts, axis=-1)
    pe_blk = jnp.concatenate([zpad(pe_raw, ROPE_LANE0), pe_raw,
                              zpad(pe_raw, HEAD_BLOCK - ROPE_LANE0 - QK_ROPE)], axis=-1)
    w_in_p = jnp.concatenate([q_a, c_raw, pe_blk, glu_a, glu_b], axis=-1)
    assert w_in_p.shape == (d_model, D_IN_PAD)

    qk = QK_NOPE + QK_ROPE
    w_q = w_q_b[0].reshape(Q_LORA, N_HEADS, qk)
    w_q = jnp.concatenate([w_q, zpad(w_q, HEAD_BLOCK - qk)], axis=-1).reshape(Q_LORA, N_HEADS * HEAD_BLOCK)
    w_k = jnp.concatenate([w_uk[0], zpad(w_uk[0], HEAD_BLOCK - QK_NOPE)], axis=-1)
    w_k = w_k.reshape(KV_LORA, N_HEADS * HEAD_BLOCK)
    w_v = w_uv[0].reshape(KV_LORA, N_HEADS * V_HEAD)
    w_uk_t = jnp.transpose(w_uk[0], (1, 2, 0))
    eye = jnp.eye(N_HEADS, dtype=w_uv.dtype)
    w_uv_pad = jnp.einsum("lhv,hg->hlgv", w_uv[0], eye).reshape(N_HEADS, KV_LORA, N_HEADS * V_HEAD)

    bf = lambda a: a.astype(BF16)
    return {
        "w_in": bf(w_in_p), "g_q_a": g_q_a, "w_q": bf(w_q), "g_kv_a": g_kv_a,
        "w_k": bf(w_k), "w_v": bf(w_v), "w_uk_t": bf(w_uk_t), "w_uv_pad": bf(w_uv_pad),
        "w_dw": w_dw[0], "b_dw": b_dw, "g_cln": g_cln, "b_cln": b_cln, "w_out": bf(w_out[0]),
        "w_xq": bf(w_xq[0]), "w_xo": bf(w_xo[0]), "w_up": bf(w_up[0]), "w_fdw": w_fdw[0],
        "b_fdw": b_fdw, "w_down": bf(w_down[0]),
        "g_pre_mix": g_pre_mix, "g_post_mix": g_post_mix, "g_pre_x": g_pre_x, "g_post_x": g_post_x,
        "g_pre_ffn": g_pre_ffn, "g_post_ffn": g_post_ffn,
    }


TM_MIX = 512
TQ_FLASH = 512
T_POST = 512
T_FFN = 256
CONV_ROWS = 64
POST_PIECES = 2
SAMPLE_SEQS = 16
SAMPLE_SEQS_POST = 8
PAGES_PER_CHUNK = 64
SUB_KEYS = 2048


def kernel(x_prompt, x_sample, cache_kv_latent, cache_k_rope, cache_mem_k, cache_mem_v, state_conv, state_ffn_conv, page_table, mem_prompt, w_in, g_q_a, w_q_b, g_kv_a, w_uk, w_uv, w_dw, b_dw, g_cln, b_cln, w_out, g_mem, w_xq, w_xk, w_xv, w_xo, w_up, w_fdw, b_fdw, w_down, g_pre_mix, g_post_mix, g_pre_x, g_post_x, g_pre_ffn, g_post_ffn):
    B, S, D = x_prompt.shape
    DB, DS, _ = x_sample.shape
    past_len = page_table.shape[1] * PAGE_SIZE
    wts = _prepare_weights(w_in, g_q_a, w_q_b, g_kv_a, w_uk, w_uv, w_dw, b_dw, g_cln, b_cln, w_out,
                           w_xq, w_xo, w_up, w_fdw, b_fdw, w_down,
                           g_pre_mix, g_post_mix, g_pre_x, g_post_x, g_pre_ffn, g_post_ffn)

    mk_p, mv_p, mkb_p, mvb_p = _mem_kv(mem_prompt, g_mem, w_xk[0].astype(BF16), w_xv[0].astype(BF16))
    tables_p = _rope_tables(S, 0, S)
    q, k, v, c_p, pe_p, u_p = _mix_in(x_prompt, tables_p, lambda b, i: (i, 0), wts, tm=TM_MIX, absorb=False)
    attn_p = _flash(q, k, v, tq=TQ_FLASH)
    halo_blocks = T_POST // CONV_HALO
    x2_p, cstate_p = _post_mix(
        x_prompt, attn_p, u_p, u_p, lambda b, i: (b, jnp.maximum(i * halo_blocks - 1, 0), 0),
        mkb_p, mvb_p, wts, G=1, T=T_POST, fresh_sequence=True, conv_rows=CONV_ROWS, n_pieces=POST_PIECES)
    y_p, fstate_p = _ffn(x2_p, jnp.zeros((B, FFN_CONV_WIDTH - 1, D_FF), F32), wts, G=1, T=T_FFN)

    rows = SAMPLE_SEQS * DS
    tables_s = _rope_tables(rows, past_len, DS)
    q_s, qlat_s, c_s, pe_s, u_s = _mix_in(
        x_sample.reshape(1, DB * DS, D), tables_s, lambda b, i: (0, 0), wts, tm=rows, absorb=True)
    qlat_s = qlat_s.reshape(DB, DS, N_HEADS, KV_LORA).transpose(0, 2, 1, 3).reshape(DB, N_HEADS * DS, KV_LORA)
    qpe_s = q_s.reshape(DB, DS, N_HEADS, HEAD_BLOCK)[..., ROPE_LANE0:ROPE_LANE0 + QK_ROPE]
    qpe_s = qpe_s.transpose(0, 2, 1, 3).reshape(DB, N_HEADS * DS, QK_ROPE)
    c_s = c_s.reshape(DB, DS, KV_LORA)
    pe_s = pe_s.reshape(DB, DS, QK_ROPE)
    attn_s = _paged_attention(page_table, qlat_s, qpe_s, c_s, pe_s, wts["w_uv_pad"],
                              cache_kv_latent.reshape(cache_kv_latent.shape[1:]),
                              jnp.swapaxes(cache_k_rope.reshape(cache_k_rope.shape[1:]), 1, 2),
                              pages_per_chunk=PAGES_PER_CHUNK, sub_keys=SUB_KEYS)
    n_state = CONV_WIDTH - 1
    halo_s = jnp.pad(state_conv[0], ((0, 0), (CONV_HALO - n_state, 0), (0, 0)))
    mem_cols = X_HEADS * X_HEAD_DIM
    x2_s, cstate_s = _post_mix(
        x_sample, attn_s, u_s.reshape(DB, DS, D_CONV), halo_s, lambda b, i: (b, 0, 0),
        cache_mem_k.reshape(DB, -1, mem_cols).astype(BF16), cache_mem_v.reshape(DB, -1, mem_cols).astype(BF16), wts,
        G=SAMPLE_SEQS_POST, T=DS, fresh_sequence=False, conv_rows=DS, n_pieces=POST_PIECES)
    y_s, fstate_s = _ffn(x2_s, state_ffn_conv[0], wts, G=SAMPLE_SEQS, T=DS)

    mem_shape = (1, B, mem_prompt.shape[1], X_HEADS, X_HEAD_DIM)
    return (y_p, y_s, c_p[None], pe_p[None], cstate_p[None], fstate_p[None],
            mk_p.reshape(mem_shape), mv_p.reshape(mem_shape),
            c_s[None], pe_s[None], cstate_s[None], fstate_s[None])
```

```python
import functools
import math

import numpy as np
import jax
import jax.numpy as jnp
from jax import lax
from jax.experimental import pallas as pl
from jax.experimental.pallas import tpu as pltpu

F32 = jnp.float32
BF16 = jnp.bfloat16

PAGE_SIZE = 128
N_HEADS = 8
QK_NOPE = 64
QK_ROPE = 32
V_HEAD = 64
Q_LORA = 384
KV_LORA = 256
D_CONV = 512
CONV_WIDTH = 31
X_HEADS = 4
X_HEAD_DIM = 256
D_FF = 2816
FFN_CONV_WIDTH = 3
ROPE_THETA = 10000.0
EPS = 1e-6
NEG_INF = -1e30
LOG2E = math.log2(math.e)
ATTN_SCALE = (QK_NOPE + QK_ROPE) ** -0.5 * LOG2E
X_SCALE = X_HEAD_DIM ** -0.5 * LOG2E

LANES = 128
SUBLANES = 8
VMEM_LIMIT = 56 * 1024 * 1024

HEAD_BLOCK = LANES
ROPE_LANE0 = QK_NOPE
ROPE_HALF = QK_ROPE // 2
CONV_HALO = 32
FFN_HALO = SUBLANES
ROPE_TABLE_ROWS = 256
ONES_ROWS = 16
N_SLOTS = 4

_C_Q = 0
_C_C = _C_Q + Q_LORA
_C_PE = _C_C + KV_LORA
_C_GA = _C_PE + HEAD_BLOCK
_C_GB = _C_GA + D_CONV
D_IN_PAD = _C_GB + D_CONV


def _params(*sem):
    return pltpu.CompilerParams(dimension_semantics=sem, vmem_limit_bytes=VMEM_LIMIT)


def _full(shape):
    n = len(shape)
    return pl.BlockSpec(shape, lambda *_: (0,) * n)


def _rms(x, g):
    return x * lax.rsqrt(jnp.mean(x * x, axis=-1, keepdims=True) + EPS) * g


def _dot(a, b):
    return jnp.dot(a, b, preferred_element_type=F32)


def _dot_nt(a, b):
    return lax.dot_general(a, b, (((1,), (1,)), ((), ())), preferred_element_type=F32)


def _rope_table_kernel(inv_ref, cos_ref, sin_a_ref, sin_b_ref, *, pos0, period):
    n = cos_ref.shape[0]
    row = pl.program_id(0) * n + lax.broadcasted_iota(jnp.int32, (n, LANES), 0)
    lane = lax.broadcasted_iota(jnp.int32, (n, LANES), 1)
    pos = (pos0 + row % period).astype(F32)
    ang = pos * inv_ref[...]
    cos, sin = jnp.cos(ang), jnp.sin(ang)
    first = (lane >= ROPE_LANE0) & (lane < ROPE_LANE0 + ROPE_HALF)
    second = (lane >= ROPE_LANE0 + ROPE_HALF) & (lane < ROPE_LANE0 + QK_ROPE)
    cos_ref[...] = jnp.where(first | second, cos, 1.0)
    sin_a_ref[...] = jnp.where(first, -sin, 0.0)
    sin_b_ref[...] = jnp.where(second, sin, 0.0)


def _rope_tables(n_rows, pos0, period):
    inv = np.zeros((1, LANES), np.float32)
    freq = ROPE_THETA ** (-np.arange(ROPE_HALF, dtype=np.float64) / ROPE_HALF)
    inv[0, ROPE_LANE0:ROPE_LANE0 + ROPE_HALF] = freq
    inv[0, ROPE_LANE0 + ROPE_HALF:ROPE_LANE0 + QK_ROPE] = freq
    out = jax.ShapeDtypeStruct((n_rows, LANES), F32)
    rows = min(n_rows, ROPE_TABLE_ROWS)
    blk = pl.BlockSpec((rows, LANES), lambda i: (i, 0))
    return pl.pallas_call(
        functools.partial(_rope_table_kernel, pos0=pos0, period=period),
        grid=(n_rows // rows,),
        in_specs=[_full(inv.shape)],
        out_specs=(blk, blk, blk),
        out_shape=(out, out, out),
        compiler_params=_params("parallel"),
        name="rope_tables",
    )(jnp.asarray(inv))


def _rope_block(x, cos, sin_a, sin_b):
    left = pltpu.roll(x, LANES - ROPE_HALF, 1)
    right = pltpu.roll(x, ROPE_HALF, 1)
    return x * cos + left * sin_a + right * sin_b


def _mix_in_kernel(x_ref, cos_ref, sin_a_ref, sin_b_ref, g_pre_ref, w_in_ref, g_q_ref, w_q_ref,
                   g_kv_ref, w_a_ref, w_b_ref, *out_refs, absorb):
    x = x_ref[0]
    h = _rms(x, g_pre_ref[...]).astype(BF16)
    proj = _dot(h, w_in_ref[...])
    cos, sin_a, sin_b = cos_ref[...], sin_a_ref[...], sin_b_ref[...]

    qn = _rms(proj[:, _C_Q:_C_C], g_q_ref[...]).astype(BF16)
    q = _dot(qn, w_q_ref[...])
    c_new = _rms(proj[:, _C_C:_C_PE], g_kv_ref[...])
    pe_blk = _rope_block(proj[:, _C_PE:_C_GA], cos, sin_a, sin_b)
    u = proj[:, _C_GA:_C_GB] * jax.nn.sigmoid(proj[:, _C_GB:])

    if absorb:
        q_ref, qlat_ref, c_ref, pe_ref, u_ref = out_refs
    else:
        q_ref, k_ref, v_ref, c_ref, pe_ref, u_ref = out_refs
        cb = c_new.astype(BF16)
        k = _dot(cb, w_a_ref[...])
        v_ref[0] = _dot(cb, w_b_ref[...]).astype(BF16)

    for hd in range(N_HEADS):
        blk = slice(hd * HEAD_BLOCK, (hd + 1) * HEAD_BLOCK)
        q_h = (_rope_block(q[:, blk], cos, sin_a, sin_b) * ATTN_SCALE).astype(BF16)
        q_ref[0, :, blk] = q_h
        if absorb:
            qlat_ref[0, :, hd * KV_LORA:(hd + 1) * KV_LORA] = _dot(
                q_h[:, :QK_NOPE], w_a_ref[hd]).astype(BF16)
        else:
            k_ref[0, :, blk] = (k[:, blk] + pe_blk).astype(BF16)

    c_ref[0] = c_new
    pe_ref[0] = pe_blk[:, ROPE_LANE0:ROPE_LANE0 + QK_ROPE]
    u_ref[0] = u


def _mix_in(x, tables, table_map, wts, *, tm, absorb):
    B, S, D = x.shape
    grid = (B, S // tm)
    row = lambda b, i: (b, i, 0)
    tab = pl.BlockSpec((tm, LANES), table_map)
    w_a, w_b = (wts["w_uk_t"], wts["w_uv_pad"]) if absorb else (wts["w_k"], wts["w_v"])
    ins = [x, *tables, wts["g_pre_mix"], wts["w_in"], wts["g_q_a"], wts["w_q"], wts["g_kv_a"], w_a, w_b]
    in_specs = [pl.BlockSpec((1, tm, D), row), tab, tab, tab] + [_full(a.shape) for a in ins[4:]]
    hq = N_HEADS * HEAD_BLOCK
    out = [((B, S, hq), BF16)]
    if absorb:
        out += [((B, S, N_HEADS * KV_LORA), BF16)]
    else:
        out += [((B, S, hq), BF16), ((B, S, N_HEADS * V_HEAD), BF16)]
    out += [((B, S, KV_LORA), F32), ((B, S, QK_ROPE), F32), ((B, S, D_CONV), F32)]
    return pl.pallas_call(
        functools.partial(_mix_in_kernel, absorb=absorb),
        grid=grid,
        in_specs=in_specs,
        out_specs=[pl.BlockSpec((1, tm, s[-1]), row) for s, _ in out],
        out_shape=[jax.ShapeDtypeStruct(s, d) for s, d in out],
        compiler_params=_params("parallel", "arbitrary"),
        name="mix_in_sample" if absorb else "mix_in_prompt",
    )(*ins)


def _transpose_bf16(x):
    return x.astype(F32).T.astype(BF16)


def _flash_kernel(q_ref, k_ref, v_ref, o_ref, vt_scr, s0_scr, m_scr, acc_scr, *, tq):
    qi = pl.program_id(2)
    n_kv = vt_scr.shape[0]

    @pl.when(qi == 0)
    def _():
        ones = jnp.ones((ONES_ROWS, tq), BF16)

        def put(t, _):
            rows = pl.ds(pl.multiple_of(t * tq, tq), tq)
            v_t = v_ref[0, rows, :].astype(F32).T
            vt_scr[t, 0] = jnp.concatenate([v_t[:V_HEAD].astype(BF16), ones], axis=0)
            vt_scr[t, 1] = jnp.concatenate([v_t[V_HEAD:].astype(BF16), ones], axis=0)
            return 0
        lax.fori_loop(0, n_kv, put, 0)

    key = lax.broadcasted_iota(jnp.int32, (tq, tq), 0)
    qry = lax.broadcasted_iota(jnp.int32, (tq, tq), 1)
    blks = [slice(j * HEAD_BLOCK, (j + 1) * HEAD_BLOCK) for j in range(2)]
    q_t = [_transpose_bf16(q_ref[0, :, blk]) for blk in blks]

    def scores(t, j):
        rows = pl.ds(pl.multiple_of(t * tq, tq), tq)
        return _dot(k_ref[0, rows, blks[j]], q_t[j])

    def absorb_tile(t, j, s, masked):
        if masked:
            s = jnp.where(key <= qry, s, NEG_INF)
        m = m_scr[j]
        m_new = jnp.maximum(m, jnp.max(s, axis=0, keepdims=True))
        alpha = jnp.exp2(m - m_new)
        p = jnp.exp2(s - m_new)
        m_scr[j] = m_new
        acc_scr[j] = alpha * acc_scr[j] + _dot(vt_scr[t, j], p.astype(BF16))

    m_scr[...] = jnp.full(m_scr.shape, NEG_INF, F32)
    acc_scr[...] = jnp.zeros(acc_scr.shape, F32)

    s0_scr[...] = scores(0, 0)

    def full_tile(t):
        s1 = scores(t, 1)
        absorb_tile(t, 0, s0_scr[...], False)
        s0_scr[...] = scores(t + 1, 0)
        absorb_tile(t, 1, s1, False)

    def two_tiles(tp, _):
        full_tile(2 * tp)
        full_tile(2 * tp + 1)
        return 0

    lax.fori_loop(0, qi // 2, two_tiles, 0)

    @pl.when(qi % 2 == 1)
    def _():
        full_tile(qi - 1)

    s1 = scores(qi, 1)
    absorb_tile(qi, 0, s0_scr[...], True)
    absorb_tile(qi, 1, s1, True)
    heads = [acc_scr[j, :V_HEAD, :] / acc_scr[j, V_HEAD:V_HEAD + 1, :] for j in range(2)]
    o_ref[0] = jnp.concatenate(heads, axis=0).T.astype(BF16)


def _flash(q, k, v, *, tq):
    B, S, _ = q.shape
    pairs = N_HEADS // 2
    return pl.pallas_call(
        functools.partial(_flash_kernel, tq=tq),
        grid=(B, pairs, S // tq),
        in_specs=[
            pl.BlockSpec((1, tq, 2 * HEAD_BLOCK), lambda b, p, i: (b, i, p)),
            pl.BlockSpec((1, S, 2 * HEAD_BLOCK), lambda b, p, i: (b, 0, p)),
            pl.BlockSpec((1, S, 2 * V_HEAD), lambda b, p, i: (b, 0, p)),
        ],
        out_specs=pl.BlockSpec((1, tq, 2 * V_HEAD), lambda b, p, i: (b, i, p)),
        out_shape=jax.ShapeDtypeStruct((B, S, N_HEADS * V_HEAD), BF16),
        scratch_shapes=[
            pltpu.VMEM((S // tq, 2, V_HEAD + ONES_ROWS, tq), BF16),
            pltpu.VMEM((tq, tq), F32),
            pltpu.VMEM((2, 1, tq), F32),
            pltpu.VMEM((2, V_HEAD + ONES_ROWS, tq), F32),
        ],
        compiler_params=_params("parallel", "parallel", "arbitrary"),
        name="flash_prompt",
    )(q, k, v)


def _paged_kernel(pt_ref, qlat_ref, qpe_ref, cnew_ref, penew_ref, wuv_ref, ckv_hbm, kr_hbm, o_ref,
                  cbuf, pbuf, sem, cb16, *, pages_per_chunk, sub_keys):
    b = pl.program_id(0)
    n_chunks = pl.num_programs(0) * N_SLOTS
    n_rows = qlat_ref.shape[1]
    chunk_keys = pages_per_chunk * PAGE_SIZE

    def page_copies(chunk, sl, j):
        page = pt_ref[chunk * pages_per_chunk + j]
        keys = pl.ds(j * PAGE_SIZE, PAGE_SIZE)
        return (pltpu.make_async_copy(ckv_hbm.at[page], cbuf.at[sl, keys], sem.at[sl]),
                pltpu.make_async_copy(kr_hbm.at[page], pbuf.at[sl, :, keys], sem.at[sl]))

    def start_chunk(chunk, sl):
        for j in range(pages_per_chunk):
            for cp in page_copies(chunk, sl, j):
                cp.start()

    def wait_chunk(chunk, sl):
        for j in range(pages_per_chunk):
            for cp in page_copies(chunk, sl, j):
                cp.wait()

    @pl.when(b == 0)
    def _():
        for k in range(N_SLOTS - 1):
            start_chunk(k, k)

    qlat = qlat_ref[0]
    qpe = qpe_ref[0]

    def update(carry, s, vals):
        m, l, acc = carry
        m_new = jnp.maximum(m, jnp.max(s, axis=-1, keepdims=True))
        alpha = jnp.exp2(m - m_new)
        p = jnp.exp2(s - m_new)
        l = alpha * l + jnp.sum(p, axis=-1, keepdims=True)
        acc = alpha * acc + _dot(p.astype(BF16), vals)
        return m_new, l, acc

    n_sub = chunk_keys // sub_keys
    carry = (jnp.full((n_rows, 1), NEG_INF, F32), jnp.zeros((n_rows, 1), F32), jnp.zeros((n_rows, KV_LORA), F32))
    for sl in range(N_SLOTS):
        chunk = b * N_SLOTS + sl
        wait_chunk(chunk, sl)
        nxt = chunk + (N_SLOTS - 1)
        nxt = jnp.where(nxt >= n_chunks, nxt - n_chunks, nxt)
        start_chunk(nxt, (sl + N_SLOTS - 1) % N_SLOTS)

        def scores(i):
            keys = slice(i * sub_keys, (i + 1) * sub_keys)
            cb16[keys, :] = cbuf[sl, keys, :].astype(BF16)
            return _dot_nt(qlat, cb16[keys, :]) + _dot(qpe, pbuf[sl, :, keys].astype(BF16))

        s = scores(0)
        for i in range(n_sub):
            s_next = scores(i + 1) if i + 1 < n_sub else None
            carry = update(carry, s, cb16[i * sub_keys:(i + 1) * sub_keys, :])
            s = s_next

    n_new = cnew_ref.shape[1]
    cb = cnew_ref[0].astype(BF16)
    pb = penew_ref[0].astype(BF16)
    s = _dot_nt(qlat, cb) + _dot_nt(qpe, pb)
    q_s = lax.broadcasted_iota(jnp.int32, (n_rows, n_new), 0) % n_new
    t = lax.broadcasted_iota(jnp.int32, (n_rows, n_new), 1)
    s = jnp.where(q_s >= t, s, NEG_INF)
    _, l, acc = update(carry, s, cb)
    o_lat = acc / l
    out = jnp.zeros((n_new, N_HEADS * V_HEAD), F32)
    for hd in range(N_HEADS):
        out = out + _dot(o_lat[hd * n_new:(hd + 1) * n_new].astype(BF16), wuv_ref[hd])
    o_ref[0] = out

    @pl.when(b == pl.num_programs(0) - 1)
    def _():
        for k in range(N_SLOTS - 1):
            wait_chunk(k, k)


def _paged_attention(page_table, qlat, qpe, c_new, pe_new, w_uv_pad, cache_c, cache_pe_t,
                     *, pages_per_chunk, sub_keys):
    DB, n_rows, _ = qlat.shape
    n_new = c_new.shape[1]
    n_pages = page_table.shape[1]
    assert n_pages == N_SLOTS * pages_per_chunk
    chunk_keys = pages_per_chunk * PAGE_SIZE
    per_b = lambda b, pt: (b, 0, 0)
    grid_spec = pltpu.PrefetchScalarGridSpec(
        num_scalar_prefetch=1,
        grid=(DB,),
        in_specs=[
            pl.BlockSpec((1, n_rows, KV_LORA), per_b),
            pl.BlockSpec((1, n_rows, QK_ROPE), per_b),
            pl.BlockSpec((1, n_new, KV_LORA), per_b),
            pl.BlockSpec((1, n_new, QK_ROPE), per_b),
            pl.BlockSpec(w_uv_pad.shape, lambda b, pt: (0, 0, 0)),
            pl.BlockSpec(memory_space=pl.ANY),
            pl.BlockSpec(memory_space=pl.ANY),
        ],
        out_specs=pl.BlockSpec((1, n_new, N_HEADS * V_HEAD), per_b),
        scratch_shapes=[
            pltpu.VMEM((N_SLOTS, chunk_keys, KV_LORA), F32),
            pltpu.VMEM((N_SLOTS, QK_ROPE, chunk_keys), F32),
            pltpu.SemaphoreType.DMA((N_SLOTS,)),
            pltpu.VMEM((chunk_keys, KV_LORA), BF16),
        ],
    )
    return pl.pallas_call(
        functools.partial(_paged_kernel, pages_per_chunk=pages_per_chunk, sub_keys=sub_keys),
        grid_spec=grid_spec,
        out_shape=jax.ShapeDtypeStruct((DB, n_new, N_HEADS * V_HEAD), F32),
        compiler_params=_params("arbitrary"),
        name="paged_sample",
    )(page_table.reshape(-1), qlat, qpe, c_new, pe_new, w_uv_pad, cache_c, cache_pe_t)


def _mem_kv_kernel(mem_ref, g_ref, w_k_ref, w_v_ref, k_ref, v_ref, kb_ref, vb_ref):
    m = _rms(mem_ref[0], g_ref[...]).astype(BF16)
    k = _dot(m, w_k_ref[...])
    v = _dot(m, w_v_ref[...])
    k_ref[0], v_ref[0] = k, v
    kb_ref[0], vb_ref[0] = k.astype(BF16), v.astype(BF16)


def _mem_kv(mem, g_mem, w_xk, w_xv):
    B, N, D = mem.shape
    blk = pl.BlockSpec((1, N, D), lambda b: (b, 0, 0))
    cols = X_HEADS * X_HEAD_DIM
    out_blk = pl.BlockSpec((1, N, cols), lambda b: (b, 0, 0))
    out = jax.ShapeDtypeStruct((B, N, cols), F32)
    out_b = jax.ShapeDtypeStruct((B, N, cols), BF16)
    return pl.pallas_call(
        _mem_kv_kernel,
        grid=(B,),
        in_specs=[blk, _full(g_mem.shape), _full(w_xk.shape), _full(w_xv.shape)],
        out_specs=[out_blk] * 4,
        out_shape=[out, out, out_b, out_b],
        compiler_params=_params("parallel"),
        name="mem_kv",
    )(mem, g_mem, w_xk, w_xv)


def _pieces(G, T, n_pieces):
    if G == 1:
        return [([0], p * (T // n_pieces), T // n_pieces) for p in range(n_pieces)]
    per = G // n_pieces
    return [(list(range(p * per, (p + 1) * per)), 0, T) for p in range(n_pieces)]


def _rows_of(ref, piece):
    gs, r0, nr = piece
    parts = [ref[g, r0:r0 + nr, :] for g in gs]
    return jnp.concatenate(parts, axis=0) if len(parts) > 1 else parts[0]


def _store_rows(ref, piece, val):
    gs, r0, nr = piece
    for gi, g in enumerate(gs):
        ref[g, r0:r0 + nr, :] = val[gi * nr:(gi + 1) * nr]


def _post_mix_kernel(x_ref, attn_ref, u_ref, halo_ref, mk_ref, mv_ref, w_dw_ref, b_dw_ref, g_cln_ref,
                     b_cln_ref, w_out_ref, g_post_mix_ref, g_pre_x_ref, w_xq_ref, w_xo_ref,
                     g_post_x_ref, x2_ref, cstate_ref, full_scr, shift_scr, *, fresh_sequence, conv_rows,
                     n_pieces):
    G, T, D = x_ref.shape
    i = pl.program_id(1)
    halo = halo_ref[...]
    if fresh_sequence:
        halo = jnp.where(i == 0, 0.0, halo)
    full_scr[:, :CONV_HALO, :] = halo
    full_scr[:, CONV_HALO:, :] = u_ref[...]
    cstate_ref[...] = full_scr[:, T + CONV_HALO - (CONV_WIDTH - 1):, :]
    n_shift = shift_scr.shape[2]
    for j in range(1, SUBLANES):
        shift_scr[j - 1] = full_scr[:, j:j + n_shift, :]

    pieces = _pieces(G, T, n_pieces)

    def conv_group(piece):
        gs, r0, nr = piece
        base = CONV_HALO - (CONV_WIDTH - 1)
        parts = []
        for g in gs:
            for c0 in range(r0, r0 + nr, conv_rows):
                acc = jnp.broadcast_to(b_dw_ref[...], (conv_rows, D_CONV))
                for k in range(CONV_WIDTH):
                    a, j = divmod(base + k, SUBLANES)
                    rows = slice(c0 + a * SUBLANES, c0 + a * SUBLANES + conv_rows)
                    src = full_scr[g, rows, :] if j == 0 else shift_scr[j - 1, g, rows, :]
                    acc = acc + w_dw_ref[k:k + 1, :] * src
                parts.append(acc)
        cv = jnp.concatenate(parts, axis=0) if len(parts) > 1 else parts[0]
        xc = cv - jnp.mean(cv, axis=-1, keepdims=True)
        var = jnp.mean(xc * xc, axis=-1, keepdims=True)
        cv = xc * lax.rsqrt(var + EPS) * g_cln_ref[...] + b_cln_ref[...]
        return (cv * jax.nn.sigmoid(cv)).astype(BF16)

    d_attn = attn_ref.shape[2]
    cvs = [conv_group(pc) for pc in pieces]
    mixes = [_dot(_rows_of(attn_ref, pc).astype(BF16), w_out_ref[:d_attn, :]) + _dot(cv, w_out_ref[d_attn:, :])
             for pc, cv in zip(pieces, cvs)]
    x1s = [_rows_of(x_ref, pc) + _rms(mix, g_post_mix_ref[...]) for pc, mix in zip(pieces, mixes)]

    qs = [_dot(_rms(x1, g_pre_x_ref[...]).astype(BF16), w_xq_ref[...]) * X_SCALE for x1 in x1s]
    os_ = []
    for (gs, r0, nr), q in zip(pieces, qs):
        heads = [(gi, g, hd, slice(hd * X_HEAD_DIM, (hd + 1) * X_HEAD_DIM))
                 for gi, g in enumerate(gs) for hd in range(X_HEADS)]
        scores = [_dot_nt(q[gi * nr:(gi + 1) * nr, cols].astype(BF16), mk_ref[g, :, cols])
                  for gi, g, hd, cols in heads]
        outs = []
        for (gi, g, hd, cols), s in zip(heads, scores):
            p = jnp.exp2(s - jnp.max(s, axis=-1, keepdims=True))
            l = jnp.sum(p, axis=-1, keepdims=True)
            outs.append(_dot(p.astype(BF16), mv_ref[g, :, cols]) / l)
        o_rows = [jnp.concatenate(outs[gi * X_HEADS:(gi + 1) * X_HEADS], axis=-1) for gi in range(len(gs))]
        os_.append(jnp.concatenate(o_rows, axis=0) if len(o_rows) > 1 else o_rows[0])
    xas = [_dot(o.astype(BF16), w_xo_ref[...]) for o in os_]
    for pc, x1, xa in zip(pieces, x1s, xas):
        _store_rows(x2_ref, pc, x1 + _rms(xa, g_post_x_ref[...]))


def _post_mix(x, attn, u, halo, halo_map, mk, mv, wts, *, G, T, fresh_sequence, conv_rows, n_pieces):
    B, S, D = x.shape
    grid = (B // G, S // T)
    row = lambda b, i: (b, i, 0)
    per_b = lambda b, i: (b, 0, 0)
    names = ["w_dw", "b_dw", "g_cln", "b_cln", "w_out", "g_post_mix", "g_pre_x", "w_xq", "w_xo", "g_post_x"]
    w = [wts[n] for n in names]
    n_state = CONV_WIDTH - 1
    return pl.pallas_call(
        functools.partial(_post_mix_kernel, fresh_sequence=fresh_sequence, conv_rows=conv_rows,
                          n_pieces=n_pieces),
        grid=grid,
        in_specs=[
            pl.BlockSpec((G, T, D), row),
            pl.BlockSpec((G, T, attn.shape[2]), row),
            pl.BlockSpec((G, T, D_CONV), row),
            pl.BlockSpec((G, CONV_HALO, D_CONV), halo_map),
            pl.BlockSpec((G,) + mk.shape[1:], per_b),
            pl.BlockSpec((G,) + mv.shape[1:], per_b),
        ] + [_full(a.shape) for a in w],
        out_specs=[pl.BlockSpec((G, T, D), row), pl.BlockSpec((G, n_state, D_CONV), per_b)],
        out_shape=[jax.ShapeDtypeStruct((B, S, D), F32), jax.ShapeDtypeStruct((B, n_state, D_CONV), F32)],
        scratch_shapes=[pltpu.VMEM((G, T + CONV_HALO, D_CONV), F32),
                        pltpu.VMEM((SUBLANES - 1, G, T + CONV_HALO - SUBLANES, D_CONV), F32)],
        compiler_params=_params("parallel", "arbitrary"),
        name="post_mix_sample" if G > 1 else "post_mix_prompt",
    )(x, attn, u, halo, mk, mv, *w)


def _ffn_kernel(x_ref, state_ref, g_pre_ref, w_up_ref, w_fdw_ref, b_fdw_ref, w_down_ref, g_post_ref,
                y_ref, fstate_ref, gate_scr, *, n_pieces):
    G, T, D = x_ref.shape
    i = pl.program_id(1)
    n_hist = FFN_CONV_WIDTH - 1
    hist = slice(FFN_HALO - n_hist, FFN_HALO)

    @pl.when(i == 0)
    def _():
        gate_scr[:, hist, :] = state_ref[...]

    pieces = _pieces(G, T, n_pieces)
    xs = [_rows_of(x_ref, pc) for pc in pieces]
    ups = [_dot(_rms(x, g_pre_ref[...]).astype(BF16), w_up_ref[...]) for x in xs]
    acts = []
    for (gs, r0, nr), up in zip(pieces, ups):
        parts = []
        for gi, g in enumerate(gs):
            gate_scr[g, FFN_HALO + r0:FFN_HALO + r0 + nr, :] = up[gi * nr:(gi + 1) * nr, D_FF:]
            gate = jnp.broadcast_to(b_fdw_ref[...], (nr, D_FF))
            for k in range(FFN_CONV_WIDTH):
                off = FFN_HALO - n_hist + k + r0
                gate = gate + w_fdw_ref[k:k + 1, :] * gate_scr[g, off:off + nr, :]
            parts.append(gate)
        gate = jnp.concatenate(parts, axis=0) if len(parts) > 1 else parts[0]
        acts.append((gate * jax.nn.sigmoid(gate) * up[:, :D_FF]).astype(BF16))
    fs = [_dot(act, w_down_ref[...]) for act in acts]
    for pc, x, f in zip(pieces, xs, fs):
        _store_rows(y_ref, pc, x + _rms(f, g_post_ref[...]))

    new_state = gate_scr[:, T + FFN_HALO - n_hist:, :]
    fstate_ref[...] = new_state
    gate_scr[:, hist, :] = new_state


def _ffn(x, state, wts, *, G, T, n_pieces):
    B, S, D = x.shape
    row = lambda b, i: (b, i, 0)
    per_b = lambda b, i: (b, 0, 0)
    names = ["g_pre_ffn", "w_up", "w_fdw", "b_fdw", "w_down", "g_post_ffn"]
    w = [wts[n] for n in names]
    n_hist = FFN_CONV_WIDTH - 1
    return pl.pallas_call(
        functools.partial(_ffn_kernel, n_pieces=n_pieces),
        grid=(B // G, S // T),
        in_specs=[pl.BlockSpec((G, T, D), row), pl.BlockSpec((G, n_hist, D_FF), per_b)]
        + [pl.BlockSpec(a.shape, lambda b, i, n=a.ndim: (0,) * n, pipeline_mode=pl.Buffered(1)) for a in w],
        out_specs=[pl.BlockSpec((G, T, D), row), pl.BlockSpec((G, n_hist, D_FF), per_b)],
        out_shape=[jax.ShapeDtypeStruct((B, S, D), F32), jax.ShapeDtypeStruct((B, n_hist, D_FF), F32)],
        scratch_shapes=[pltpu.VMEM((G, T + FFN_HALO, D_FF), F32)],
        compiler_params=_params("parallel", "arbitrary"),
        name="ffn_sample" if G > 1 else "ffn_prompt",
    )(x, state, *w)


def _prepare_weights(w_in, g_q_a, w_q_b, g_kv_a, w_uk, w_uv, w_dw, b_dw, g_cln, b_cln, w_out,
                     w_xq, w_xo, w_up, w_fdw, b_fdw, w_down,
                     g_pre_mix, g_post_mix, g_pre_x, g_post_x, g_pre_ffn, g_post_ffn):
    assert w_in.shape[0] == 1, "one trunk layer"
    d_model = w_in.shape[1]
    zpad = lambda a, n: jnp.zeros(a.shape[:-1] + (n,), a.dtype)
    w_in0 = w_in[0]
    cuts = [Q_LORA, Q_LORA + KV_LORA, Q_LORA + KV_LORA + QK_ROPE, Q_LORA + KV_LORA + QK_ROPE + D_CONV]
    q_a, c_raw, pe_raw, glu_a, glu_b = jnp.split(w_in0, cuts, axis=-1)
    pe_blk = jnp.concatenate([zpad(pe_raw, ROPE_LANE0), pe_raw,
                              zpad(pe_raw, HEAD_BLOCK - ROPE_LANE0 - QK_ROPE)], axis=-1)
    w_in_p = jnp.concatenate([q_a, c_raw, pe_blk, glu_a, glu_b], axis=-1)
    assert w_in_p.shape == (d_model, D_IN_PAD)

    qk = QK_NOPE + QK_ROPE
    w_q = w_q_b[0].reshape(Q_LORA, N_HEADS, qk)
    w_q = jnp.concatenate([w_q, zpad(w_q, HEAD_BLOCK - qk)], axis=-1).reshape(Q_LORA, N_HEADS * HEAD_BLOCK)
    w_k = jnp.concatenate([w_uk[0], zpad(w_uk[0], HEAD_BLOCK - QK_NOPE)], axis=-1)
    w_k = w_k.reshape(KV_LORA, N_HEADS * HEAD_BLOCK)
    w_v = w_uv[0].reshape(KV_LORA, N_HEADS * V_HEAD)
    w_uk_t = jnp.transpose(w_uk[0], (1, 2, 0))
    eye = jnp.eye(N_HEADS, dtype=w_uv.dtype)
    w_uv_pad = jnp.einsum("lhv,hg->hlgv", w_uv[0], eye).reshape(N_HEADS, KV_LORA, N_HEADS * V_HEAD)

    bf = lambda a: a.astype(BF16)
    return {
        "w_in": bf(w_in_p), "g_q_a": g_q_a, "w_q": bf(w_q), "g_kv_a": g_kv_a,
        "w_k": bf(w_k), "w_v": bf(w_v), "w_uk_t": bf(w_uk_t), "w_uv_pad": bf(w_uv_pad),
        "w_dw": w_dw[0], "b_dw": b_dw, "g_cln": g_cln, "b_cln": b_cln, "w_out": bf(w_out[0]),
        "w_xq": bf(w_xq[0]), "w_xo": bf(w_xo[0]), "w_up": bf(w_up[0]), "w_fdw": w_fdw[0],
        "b_fdw": b_fdw, "w_down": bf(w_down[0]),
        "g_pre_mix": g_pre_mix, "g_post_mix": g_post_mix, "g_pre_x": g_pre_x, "g_post_x": g_post_x,
        "g_pre_ffn": g_pre_ffn, "g_post_ffn": g_post_ffn,
    }


TM_MIX = 512
TQ_FLASH = 512
T_POST = 512
T_FFN = 512
CONV_ROWS = 64
PIECES = 2
SAMPLE_SEQS = 16
SAMPLE_SEQS_POST = 8
PAGES_PER_CHUNK = 32
SUB_KEYS = 2048


def kernel(x_prompt, x_sample, cache_kv_latent, cache_k_rope, cache_mem_k, cache_mem_v, state_conv, state_ffn_conv, page_table, mem_prompt, w_in, g_q_a, w_q_b, g_kv_a, w_uk, w_uv, w_dw, b_dw, g_cln, b_cln, w_out, g_mem, w_xq, w_xk, w_xv, w_xo, w_up, w_fdw, b_fdw, w_down, g_pre_mix, g_post_mix, g_pre_x, g_post_x, g_pre_ffn, g_post_ffn):
    B, S, D = x_prompt.shape
    DB, DS, _ = x_sample.shape
    past_len = page_table.shape[1] * PAGE_SIZE
    wts = _prepare_weights(w_in, g_q_a, w_q_b, g_kv_a, w_uk, w_uv, w_dw, b_dw, g_cln, b_cln, w_out,
                           w_xq, w_xo, w_up, w_fdw, b_fdw, w_down,
                           g_pre_mix, g_post_mix, g_pre_x, g_post_x, g_pre_ffn, g_post_ffn)

    mk_p, mv_p, mkb_p, mvb_p = _mem_kv(mem_prompt, g_mem, w_xk[0].astype(BF16), w_xv[0].astype(BF16))
    tables_p = _rope_tables(S, 0, S)
    q, k, v, c_p, pe_p, u_p = _mix_in(x_prompt, tables_p, lambda b, i: (i, 0), wts, tm=TM_MIX, absorb=False)
    attn_p = _flash(q, k, v, tq=TQ_FLASH)
    halo_blocks = T_POST // CONV_HALO
    x2_p, cstate_p = _post_mix(
        x_prompt, attn_p, u_p, u_p, lambda b, i: (b, jnp.maximum(i * halo_blocks - 1, 0), 0),
        mkb_p, mvb_p, wts, G=1, T=T_POST, fresh_sequence=True, conv_rows=CONV_ROWS, n_pieces=PIECES)
    y_p, fstate_p = _ffn(x2_p, jnp.zeros((B, FFN_CONV_WIDTH - 1, D_FF), F32), wts, G=1, T=T_FFN, n_pieces=PIECES)

    rows = SAMPLE_SEQS * DS
    tables_s = _rope_tables(rows, past_len, DS)
    q_s, qlat_s, c_s, pe_s, u_s = _mix_in(
        x_sample.reshape(1, DB * DS, D), tables_s, lambda b, i: (0, 0), wts, tm=rows, absorb=True)
    qlat_s = qlat_s.reshape(DB, DS, N_HEADS, KV_LORA).transpose(0, 2, 1, 3).reshape(DB, N_HEADS * DS, KV_LORA)
    qpe_s = q_s.reshape(DB, DS, N_HEADS, HEAD_BLOCK)[..., ROPE_LANE0:ROPE_LANE0 + QK_ROPE]
    qpe_s = qpe_s.transpose(0, 2, 1, 3).reshape(DB, N_HEADS * DS, QK_ROPE)
    c_s = c_s.reshape(DB, DS, KV_LORA)
    pe_s = pe_s.reshape(DB, DS, QK_ROPE)
    attn_s = _paged_attention(page_table, qlat_s, qpe_s, c_s, pe_s, wts["w_uv_pad"],
                              cache_kv_latent.reshape(cache_kv_latent.shape[1:]),
                              jnp.swapaxes(cache_k_rope.reshape(cache_k_rope.shape[1:]), 1, 2),
                              pages_per_chunk=PAGES_PER_CHUNK, sub_keys=SUB_KEYS)
    n_state = CONV_WIDTH - 1
    halo_s = jnp.pad(state_conv[0], ((0, 0), (CONV_HALO - n_state, 0), (0, 0)))
    mem_cols = X_HEADS * X_HEAD_DIM
    x2_s, cstate_s = _post_mix(
        x_sample, attn_s, u_s.reshape(DB, DS, D_CONV), halo_s, lambda b, i: (b, 0, 0),
        cache_mem_k.astype(BF16).reshape(DB, -1, mem_cols), cache_mem_v.astype(BF16).reshape(DB, -1, mem_cols), wts,
        G=SAMPLE_SEQS_POST, T=DS, fresh_sequence=False, conv_rows=DS, n_pieces=PIECES)
    y_s, fstate_s = _ffn(x2_s, state_ffn_conv[0], wts, G=SAMPLE_SEQS, T=DS, n_pieces=PIECES)

    mem_shape = (1, B, mem_prompt.shape[1], X_HEADS, X_HEAD_DIM)
    return (y_p, y_s, c_p[None], pe_p[None], cstate_p[None], fstate_p[None],
            mk_p.reshape(mem_shape), mv_p.reshape(mem_shape),
            c_s[None], pe_s[None], cstate_s[None], fstate_s[None])
```

```python
import functools
import math

import numpy as np
import jax
import jax.numpy as jnp
from jax import lax
from jax.experimental import pallas as pl
from jax.experimental.pallas import tpu as pltpu

F32 = jnp.float32
BF16 = jnp.bfloat16

PAGE_SIZE = 128
N_HEADS = 8
QK_NOPE = 64
QK_ROPE = 32
V_HEAD = 64
Q_LORA = 384
KV_LORA = 256
D_CONV = 512
CONV_WIDTH = 31
X_HEADS = 4
X_HEAD_DIM = 256
D_FF = 2816
FFN_CONV_WIDTH = 3
ROPE_THETA = 10000.0
EPS = 1e-6
NEG_INF = -1e30
LOG2E = math.log2(math.e)
ATTN_SCALE = (QK_NOPE + QK_ROPE) ** -0.5 * LOG2E
X_SCALE = X_HEAD_DIM ** -0.5 * LOG2E

LANES = 128
SUBLANES = 8
VMEM_LIMIT = 56 * 1024 * 1024

HEAD_BLOCK = LANES
ROPE_LANE0 = QK_NOPE
ROPE_HALF = QK_ROPE // 2
CONV_HALO = 32
FFN_HALO = SUBLANES
ROPE_TABLE_ROWS = 256
ONES_ROWS = 16
N_SLOTS = 4

_C_Q = 0
_C_C = _C_Q + Q_LORA
_C_PE = _C_C + KV_LORA
_C_GA = _C_PE + HEAD_BLOCK
_C_GB = _C_GA + D_CONV
D_IN_PAD = _C_GB + D_CONV


def _params(*sem):
    return pltpu.CompilerParams(dimension_semantics=sem, vmem_limit_bytes=VMEM_LIMIT)


def _full(shape):
    n = len(shape)
    return pl.BlockSpec(shape, lambda *_: (0,) * n)


def _rms(x, g):
    return x * lax.rsqrt(jnp.mean(x * x, axis=-1, keepdims=True) + EPS) * g


def _dot(a, b):
    return jnp.dot(a, b, preferred_element_type=F32)


def _dot_nt(a, b):
    return lax.dot_general(a, b, (((1,), (1,)), ((), ())), preferred_element_type=F32)


def _rope_table_kernel(inv_ref, cos_ref, sin_a_ref, sin_b_ref, *, pos0, period):
    n = cos_ref.shape[0]
    row = pl.program_id(0) * n + lax.broadcasted_iota(jnp.int32, (n, LANES), 0)
    lane = lax.broadcasted_iota(jnp.int32, (n, LANES), 1)
    pos = (pos0 + row % period).astype(F32)
    ang = pos * inv_ref[...]
    cos, sin = jnp.cos(ang), jnp.sin(ang)
    first = (lane >= ROPE_LANE0) & (lane < ROPE_LANE0 + ROPE_HALF)
    second = (lane >= ROPE_LANE0 + ROPE_HALF) & (lane < ROPE_LANE0 + QK_ROPE)
    cos_ref[...] = jnp.where(first | second, cos, 1.0)
    sin_a_ref[...] = jnp.where(first, -sin, 0.0)
    sin_b_ref[...] = jnp.where(second, sin, 0.0)


def _rope_tables(n_rows, pos0, period):
    inv = np.zeros((1, LANES), np.float32)
    freq = ROPE_THETA ** (-np.arange(ROPE_HALF, dtype=np.float64) / ROPE_HALF)
    inv[0, ROPE_LANE0:ROPE_LANE0 + ROPE_HALF] = freq
    inv[0, ROPE_LANE0 + ROPE_HALF:ROPE_LANE0 + QK_ROPE] = freq
    out = jax.ShapeDtypeStruct((n_rows, LANES), F32)
    rows = min(n_rows, ROPE_TABLE_ROWS)
    blk = pl.BlockSpec((rows, LANES), lambda i: (i, 0))
    return pl.pallas_call(
        functools.partial(_rope_table_kernel, pos0=pos0, period=period),
        grid=(n_rows // rows,),
        in_specs=[_full(inv.shape)],
        out_specs=(blk, blk, blk),
        out_shape=(out, out, out),
        compiler_params=_params("parallel"),
        name="rope_tables",
    )(jnp.asarray(inv))


def _rope_block(x, cos, sin_a, sin_b):
    left = pltpu.roll(x, LANES - ROPE_HALF, 1)
    right = pltpu.roll(x, ROPE_HALF, 1)
    return x * cos + left * sin_a + right * sin_b


def _mix_in_kernel(x_ref, cos_ref, sin_a_ref, sin_b_ref, g_pre_ref, w_in_ref, g_q_ref, w_q_ref,
                   g_kv_ref, w_a_ref, w_b_ref, *out_refs, absorb):
    x = x_ref[0]
    h = _rms(x, g_pre_ref[...]).astype(BF16)
    proj = _dot(h, w_in_ref[...])
    cos, sin_a, sin_b = cos_ref[...], sin_a_ref[...], sin_b_ref[...]

    qn = _rms(proj[:, _C_Q:_C_C], g_q_ref[...]).astype(BF16)
    q = _dot(qn, w_q_ref[...])
    c_new = _rms(proj[:, _C_C:_C_PE], g_kv_ref[...])
    pe_blk = _rope_block(proj[:, _C_PE:_C_GA], cos, sin_a, sin_b)
    u = proj[:, _C_GA:_C_GB] * jax.nn.sigmoid(proj[:, _C_GB:])

    if absorb:
        q_ref, qlat_ref, c_ref, pe_ref, u_ref = out_refs
    else:
        q_ref, k_ref, v_ref, c_ref, pe_ref, u_ref = out_refs
        cb = c_new.astype(BF16)
        k = _dot(cb, w_a_ref[...])
        v_ref[0] = _dot(cb, w_b_ref[...]).astype(BF16)

    for hd in range(N_HEADS):
        blk = slice(hd * HEAD_BLOCK, (hd + 1) * HEAD_BLOCK)
        q_h = (_rope_block(q[:, blk], cos, sin_a, sin_b) * ATTN_SCALE).astype(BF16)
        q_ref[0, :, blk] = q_h
        if absorb:
            qlat_ref[0, :, hd * KV_LORA:(hd + 1) * KV_LORA] = _dot(
                q_h[:, :QK_NOPE], w_a_ref[hd]).astype(BF16)
        else:
            k_ref[0, :, blk] = (k[:, blk] + pe_blk).astype(BF16)

    c_ref[0] = c_new
    pe_ref[0] = pe_blk[:, ROPE_LANE0:ROPE_LANE0 + QK_ROPE]
    u_ref[0] = u


def _mix_in(x, tables, table_map, wts, *, tm, absorb):
    B, S, D = x.shape
    grid = (B, S // tm)
    row = lambda b, i: (b, i, 0)
    tab = pl.BlockSpec((tm, LANES), table_map)
    w_a, w_b = (wts["w_uk_t"], wts["w_uv_pad"]) if absorb else (wts["w_k"], wts["w_v"])
    ins = [x, *tables, wts["g_pre_mix"], wts["w_in"], wts["g_q_a"], wts["w_q"], wts["g_kv_a"], w_a, w_b]
    in_specs = [pl.BlockSpec((1, tm, D), row), tab, tab, tab] + [_full(a.shape) for a in ins[4:]]
    hq = N_HEADS * HEAD_BLOCK
    out = [((B, S, hq), BF16)]
    if absorb:
        out += [((B, S, N_HEADS * KV_LORA), BF16)]
    else:
        out += [((B, S, hq), BF16), ((B, S, N_HEADS * V_HEAD), BF16)]
    out += [((B, S, KV_LORA), F32), ((B, S, QK_ROPE), F32), ((B, S, D_CONV), F32)]
    return pl.pallas_call(
        functools.partial(_mix_in_kernel, absorb=absorb),
        grid=grid,
        in_specs=in_specs,
        out_specs=[pl.BlockSpec((1, tm, s[-1]), row) for s, _ in out],
        out_shape=[jax.ShapeDtypeStruct(s, d) for s, d in out],
        compiler_params=_params("parallel", "arbitrary"),
        name="mix_in_sample" if absorb else "mix_in_prompt",
    )(*ins)


def _transpose_bf16(x):
    return x.astype(F32).T.astype(BF16)


def _flash_kernel(q_ref, k_ref, v_ref, o_ref, vt_scr, s0_scr, m_scr, acc_scr, *, tq):
    qi = pl.program_id(2)
    n_kv = vt_scr.shape[0]

    @pl.when(qi == 0)
    def _():
        ones = jnp.ones((ONES_ROWS, tq), BF16)

        def put(t, _):
            rows = pl.ds(pl.multiple_of(t * tq, tq), tq)
            v_t = v_ref[0, rows, :].astype(F32).T
            vt_scr[t, 0] = jnp.concatenate([v_t[:V_HEAD].astype(BF16), ones], axis=0)
            vt_scr[t, 1] = jnp.concatenate([v_t[V_HEAD:].astype(BF16), ones], axis=0)
            return 0
        lax.fori_loop(0, n_kv, put, 0)

    key = lax.broadcasted_iota(jnp.int32, (tq, tq), 0)
    qry = lax.broadcasted_iota(jnp.int32, (tq, tq), 1)
    blks = [slice(j * HEAD_BLOCK, (j + 1) * HEAD_BLOCK) for j in range(2)]
    q_t = [_transpose_bf16(q_ref[0, :, blk]) for blk in blks]

    def scores(t, j):
        rows = pl.ds(pl.multiple_of(t * tq, tq), tq)
        return _dot(k_ref[0, rows, blks[j]], q_t[j])

    def absorb_tile(t, j, s, masked):
        if masked:
            s = jnp.where(key <= qry, s, NEG_INF)
        m = m_scr[j]
        m_new = jnp.maximum(m, jnp.max(s, axis=0, keepdims=True))
        alpha = jnp.exp2(m - m_new)
        p = jnp.exp2(s - m_new)
        m_scr[j] = m_new
        acc_scr[j] = alpha * acc_scr[j] + _dot(vt_scr[t, j], p.astype(BF16))

    m_scr[...] = jnp.full(m_scr.shape, NEG_INF, F32)
    acc_scr[...] = jnp.zeros(acc_scr.shape, F32)

    s0_scr[...] = scores(0, 0)

    def full_tile(t):
        s1 = scores(t, 1)
        absorb_tile(t, 0, s0_scr[...], False)
        s0_scr[...] = scores(t + 1, 0)
        absorb_tile(t, 1, s1, False)

    def two_tiles(tp, _):
        full_tile(2 * tp)
        full_tile(2 * tp + 1)
        return 0

    lax.fori_loop(0, qi // 2, two_tiles, 0)

    @pl.when(qi % 2 == 1)
    def _():
        full_tile(qi - 1)

    s1 = scores(qi, 1)
    absorb_tile(qi, 0, s0_scr[...], True)
    absorb_tile(qi, 1, s1, True)
    heads = [acc_scr[j, :V_HEAD, :] / acc_scr[j, V_HEAD:V_HEAD + 1, :] for j in range(2)]
    o_ref[0] = jnp.concatenate(heads, axis=0).T.astype(BF16)


def _flash(q, k, v, *, tq):
    B, S, _ = q.shape
    pairs = N_HEADS // 2
    return pl.pallas_call(
        functools.partial(_flash_kernel, tq=tq),
        grid=(B, pairs, S // tq),
        in_specs=[
            pl.BlockSpec((1, tq, 2 * HEAD_BLOCK), lambda b, p, i: (b, i, p)),
            pl.BlockSpec((1, S, 2 * HEAD_BLOCK), lambda b, p, i: (b, 0, p)),
            pl.BlockSpec((1, S, 2 * V_HEAD), lambda b, p, i: (b, 0, p)),
        ],
        out_specs=pl.BlockSpec((1, tq, 2 * V_HEAD), lambda b, p, i: (b, i, p)),
        out_shape=jax.ShapeDtypeStruct((B, S, N_HEADS * V_HEAD), BF16),
        scratch_shapes=[
            pltpu.VMEM((S // tq, 2, V_HEAD + ONES_ROWS, tq), BF16),
            pltpu.VMEM((tq, tq), F32),
            pltpu.VMEM((2, 1, tq), F32),
            pltpu.VMEM((2, V_HEAD + ONES_ROWS, tq), F32),
        ],
        compiler_params=_params("parallel", "parallel", "arbitrary"),
        name="flash_prompt",
    )(q, k, v)


def _paged_kernel(pt_ref, qlat_ref, qpe_ref, cnew_ref, penew_ref, wuv_ref, ckv_hbm, kr_hbm, o_ref,
                  cbuf, pbuf, sem, cb16, *, pages_per_chunk, sub_keys):
    b = pl.program_id(0)
    n_chunks = pl.num_programs(0) * N_SLOTS
    n_rows = qlat_ref.shape[1]
    chunk_keys = pages_per_chunk * PAGE_SIZE

    def page_copies(chunk, sl, j):
        page = pt_ref[chunk * pages_per_chunk + j]
        keys = pl.ds(j * PAGE_SIZE, PAGE_SIZE)
        return (pltpu.make_async_copy(ckv_hbm.at[page], cbuf.at[sl, keys], sem.at[sl]),
                pltpu.make_async_copy(kr_hbm.at[page], pbuf.at[sl, :, keys], sem.at[sl]))

    def start_chunk(chunk, sl):
        for j in range(pages_per_chunk):
            for cp in page_copies(chunk, sl, j):
                cp.start()

    def wait_chunk(chunk, sl):
        for j in range(pages_per_chunk):
            for cp in page_copies(chunk, sl, j):
                cp.wait()

    @pl.when(b == 0)
    def _():
        for k in range(N_SLOTS - 1):
            start_chunk(k, k)

    qlat = qlat_ref[0]
    qpe = qpe_ref[0]

    def update(carry, s, vals):
        m, l, acc = carry
        m_new = jnp.maximum(m, jnp.max(s, axis=-1, keepdims=True))
        alpha = jnp.exp2(m - m_new)
        p = jnp.exp2(s - m_new)
        l = alpha * l + jnp.sum(p, axis=-1, keepdims=True)
        acc = alpha * acc + _dot(p.astype(BF16), vals)
        return m_new, l, acc

    n_sub = chunk_keys // sub_keys
    carry = (jnp.full((n_rows, 1), NEG_INF, F32), jnp.zeros((n_rows, 1), F32), jnp.zeros((n_rows, KV_LORA), F32))
    for sl in range(N_SLOTS):
        chunk = b * N_SLOTS + sl
        wait_chunk(chunk, sl)
        nxt = chunk + (N_SLOTS - 1)
        nxt = jnp.where(nxt >= n_chunks, nxt - n_chunks, nxt)
        start_chunk(nxt, (sl + N_SLOTS - 1) % N_SLOTS)

        def scores(i):
            keys = slice(i * sub_keys, (i + 1) * sub_keys)
            cb16[keys, :] = cbuf[sl, keys, :].astype(BF16)
            return _dot_nt(qlat, cb16[keys, :]) + _dot(qpe, pbuf[sl, :, keys].astype(BF16))

        s = scores(0)
        for i in range(n_sub):
            s_next = scores(i + 1) if i + 1 < n_sub else None
            carry = update(carry, s, cb16[i * sub_keys:(i + 1) * sub_keys, :])
            s = s_next

    n_new = cnew_ref.shape[1]
    cb = cnew_ref[0].astype(BF16)
    pb = penew_ref[0].astype(BF16)
    s = _dot_nt(qlat, cb) + _dot_nt(qpe, pb)
    q_s = lax.broadcasted_iota(jnp.int32, (n_rows, n_new), 0) % n_new
    t = lax.broadcasted_iota(jnp.int32, (n_rows, n_new), 1)
    s = jnp.where(q_s >= t, s, NEG_INF)
    _, l, acc = update(carry, s, cb)
    o_lat = acc / l
    out = jnp.zeros((n_new, N_HEADS * V_HEAD), F32)
    for hd in range(N_HEADS):
        out = out + _dot(o_lat[hd * n_new:(hd + 1) * n_new].astype(BF16), wuv_ref[hd])
    o_ref[0] = out

    @pl.when(b == pl.num_programs(0) - 1)
    def _():
        for k in range(N_SLOTS - 1):
            wait_chunk(k, k)


def _paged_attention(page_table, qlat, qpe, c_new, pe_new, w_uv_pad, cache_c, cache_pe_t,
                     *, pages_per_chunk, sub_keys):
    DB, n_rows, _ = qlat.shape
    n_new = c_new.shape[1]
    n_pages = page_table.shape[1]
    assert n_pages == N_SLOTS * pages_per_chunk
    chunk_keys = pages_per_chunk * PAGE_SIZE
    per_b = lambda b, pt: (b, 0, 0)
    grid_spec = pltpu.PrefetchScalarGridSpec(
        num_scalar_prefetch=1,
        grid=(DB,),
        in_specs=[
            pl.BlockSpec((1, n_rows, KV_LORA), per_b),
            pl.BlockSpec((1, n_rows, QK_ROPE), per_b),
            pl.BlockSpec((1, n_new, KV_LORA), per_b),
            pl.BlockSpec((1, n_new, QK_ROPE), per_b),
            pl.BlockSpec(w_uv_pad.shape, lambda b, pt: (0, 0, 0)),
            pl.BlockSpec(memory_space=pl.ANY),
            pl.BlockSpec(memory_space=pl.ANY),
        ],
        out_specs=pl.BlockSpec((1, n_new, N_HEADS * V_HEAD), per_b),
        scratch_shapes=[
            pltpu.VMEM((N_SLOTS, chunk_keys, KV_LORA), F32),
            pltpu.VMEM((N_SLOTS, QK_ROPE, chunk_keys), F32),
            pltpu.SemaphoreType.DMA((N_SLOTS,)),
            pltpu.VMEM((chunk_keys, KV_LORA), BF16),
        ],
    )
    return pl.pallas_call(
        functools.partial(_paged_kernel, pages_per_chunk=pages_per_chunk, sub_keys=sub_keys),
        grid_spec=grid_spec,
        out_shape=jax.ShapeDtypeStruct((DB, n_new, N_HEADS * V_HEAD), F32),
        compiler_params=_params("arbitrary"),
        name="paged_sample",
    )(page_table.reshape(-1), qlat, qpe, c_new, pe_new, w_uv_pad, cache_c, cache_pe_t)


def _mem_kv_kernel(mem_ref, g_ref, w_k_ref, w_v_ref, k_ref, v_ref, kb_ref, vb_ref):
    m = _rms(mem_ref[0], g_ref[...]).astype(BF16)
    k = _dot(m, w_k_ref[...])
    v = _dot(m, w_v_ref[...])
    k_ref[0], v_ref[0] = k, v
    kb_ref[0], vb_ref[0] = k.astype(BF16), v.astype(BF16)


def _mem_kv(mem, g_mem, w_xk, w_xv):
    B, N, D = mem.shape
    blk = pl.BlockSpec((1, N, D), lambda b: (b, 0, 0))
    cols = X_HEADS * X_HEAD_DIM
    out_blk = pl.BlockSpec((1, N, cols), lambda b: (b, 0, 0))
    out = jax.ShapeDtypeStruct((B, N, cols), F32)
    out_b = jax.ShapeDtypeStruct((B, N, cols), BF16)
    return pl.pallas_call(
        _mem_kv_kernel,
        grid=(B,),
        in_specs=[blk, _full(g_mem.shape), _full(w_xk.shape), _full(w_xv.shape)],
        out_specs=[out_blk] * 4,
        out_shape=[out, out, out_b, out_b],
        compiler_params=_params("parallel"),
        name="mem_kv",
    )(mem, g_mem, w_xk, w_xv)


def _pieces(G, T, n_pieces):
    if G == 1:
        return [([0], p * (T // n_pieces), T // n_pieces) for p in range(n_pieces)]
    per = G // n_pieces
    return [(list(range(p * per, (p + 1) * per)), 0, T) for p in range(n_pieces)]


def _rows_of(ref, piece):
    gs, r0, nr = piece
    parts = [ref[g, r0:r0 + nr, :] for g in gs]
    return jnp.concatenate(parts, axis=0) if len(parts) > 1 else parts[0]


def _store_rows(ref, piece, val):
    gs, r0, nr = piece
    for gi, g in enumerate(gs):
        ref[g, r0:r0 + nr, :] = val[gi * nr:(gi + 1) * nr]


def _post_mix_kernel(x_ref, attn_ref, u_ref, halo_ref, mk_ref, mv_ref, w_dw_ref, b_dw_ref, g_cln_ref,
                     b_cln_ref, w_out_ref, g_post_mix_ref, g_pre_x_ref, w_xq_ref, w_xo_ref,
                     g_post_x_ref, x2_ref, cstate_ref, full_scr, shift_scr, *mem_scr, fresh_sequence, conv_rows,
                     n_pieces):
    G, T, D = x_ref.shape
    i = pl.program_id(1)
    if mem_scr:
        mk_buf, mv_buf, mem_sem = mem_scr
        step, n_steps = pl.program_id(0), pl.num_programs(0)
        slot = step % 2

        def mem_copies(stp, sl):
            return [pltpu.make_async_copy(src.at[stp * G + g, :, hd, :], dst.at[sl, g, hd], mem_sem.at[sl])
                    for g in range(G) for hd in range(X_HEADS) for src, dst in ((mk_ref, mk_buf), (mv_ref, mv_buf))]

        @pl.when(step == 0)
        def _():
            for cp in mem_copies(0, 0):
                cp.start()

        nxt = jnp.where(step + 1 == n_steps, 0, step + 1)
        for cp in mem_copies(nxt, 1 - slot):
            cp.start()

        def mem_tiles(g, hd, cols):
            return mk_buf[slot, g, hd].astype(BF16), mv_buf[slot, g, hd].astype(BF16)
    else:
        def mem_tiles(g, hd, cols):
            return mk_ref[g, :, cols], mv_ref[g, :, cols]
    halo = halo_ref[...]
    if fresh_sequence:
        halo = jnp.where(i == 0, 0.0, halo)
    full_scr[:, :CONV_HALO, :] = halo
    full_scr[:, CONV_HALO:, :] = u_ref[...]
    cstate_ref[...] = full_scr[:, T + CONV_HALO - (CONV_WIDTH - 1):, :]
    n_shift = shift_scr.shape[2]
    for j in range(1, SUBLANES):
        shift_scr[j - 1] = full_scr[:, j:j + n_shift, :]

    pieces = _pieces(G, T, n_pieces)

    def conv_group(piece):
        gs, r0, nr = piece
        base = CONV_HALO - (CONV_WIDTH - 1)
        parts = []
        for g in gs:
            for c0 in range(r0, r0 + nr, conv_rows):
                acc = jnp.broadcast_to(b_dw_ref[...], (conv_rows, D_CONV))
                for k in range(CONV_WIDTH):
                    a, j = divmod(base + k, SUBLANES)
                    rows = slice(c0 + a * SUBLANES, c0 + a * SUBLANES + conv_rows)
                    src = full_scr[g, rows, :] if j == 0 else shift_scr[j - 1, g, rows, :]
                    acc = acc + w_dw_ref[k:k + 1, :] * src
                parts.append(acc)
        cv = jnp.concatenate(parts, axis=0) if len(parts) > 1 else parts[0]
        xc = cv - jnp.mean(cv, axis=-1, keepdims=True)
        var = jnp.mean(xc * xc, axis=-1, keepdims=True)
        cv = xc * lax.rsqrt(var + EPS) * g_cln_ref[...] + b_cln_ref[...]
        return (cv * jax.nn.sigmoid(cv)).astype(BF16)

    d_attn = attn_ref.shape[2]
    cvs = [conv_group(pc) for pc in pieces]
    mixes = [_dot(_rows_of(attn_ref, pc).astype(BF16), w_out_ref[:d_attn, :]) + _dot(cv, w_out_ref[d_attn:, :])
             for pc, cv in zip(pieces, cvs)]
    x1s = [_rows_of(x_ref, pc) + _rms(mix, g_post_mix_ref[...]) for pc, mix in zip(pieces, mixes)]

    qs = [_dot(_rms(x1, g_pre_x_ref[...]).astype(BF16), w_xq_ref[...]) * X_SCALE for x1 in x1s]
    if mem_scr:
        for cp in mem_copies(step, slot):
            cp.wait()
    os_ = []
    for (gs, r0, nr), q in zip(pieces, qs):
        heads = [(gi, g, hd, slice(hd * X_HEAD_DIM, (hd + 1) * X_HEAD_DIM))
                 for gi, g in enumerate(gs) for hd in range(X_HEADS)]
        mems = [mem_tiles(g, hd, cols) for gi, g, hd, cols in heads]
        scores = [_dot_nt(q[gi * nr:(gi + 1) * nr, cols].astype(BF16), mk)
                  for (gi, g, hd, cols), (mk, _) in zip(heads, mems)]
        outs = []
        for (_, mv), s in zip(mems, scores):
            p = jnp.exp2(s - jnp.max(s, axis=-1, keepdims=True))
            l = jnp.sum(p, axis=-1, keepdims=True)
            outs.append(_dot(p.astype(BF16), mv) / l)
        o_rows = [jnp.concatenate(outs[gi * X_HEADS:(gi + 1) * X_HEADS], axis=-1) for gi in range(len(gs))]
        os_.append(jnp.concatenate(o_rows, axis=0) if len(o_rows) > 1 else o_rows[0])
    xas = [_dot(o.astype(BF16), w_xo_ref[...]) for o in os_]
    for pc, x1, xa in zip(pieces, x1s, xas):
        _store_rows(x2_ref, pc, x1 + _rms(xa, g_post_x_ref[...]))

    if mem_scr:
        @pl.when(step == n_steps - 1)
        def _():
            for cp in mem_copies(nxt, 1 - slot):
                cp.wait()


def _post_mix(x, attn, u, halo, halo_map, mk, mv, wts, *, G, T, fresh_sequence, conv_rows, n_pieces):
    B, S, D = x.shape
    grid = (B // G, S // T)
    mem_in_hbm = mk.ndim == 4
    if mem_in_hbm:
        mem_specs = [pl.BlockSpec(memory_space=pl.ANY)] * 2
        tiles = (2, G, X_HEADS, mk.shape[1], X_HEAD_DIM)
        mem_scratch = [pltpu.VMEM(tiles, F32), pltpu.VMEM(tiles, F32), pltpu.SemaphoreType.DMA((2,))]
    else:
        mem_specs = [pl.BlockSpec((G,) + mk.shape[1:], lambda b, i: (b, 0, 0))] * 2
        mem_scratch = []
    row = lambda b, i: (b, i, 0)
    per_b = lambda b, i: (b, 0, 0)
    names = ["w_dw", "b_dw", "g_cln", "b_cln", "w_out", "g_post_mix", "g_pre_x", "w_xq", "w_xo", "g_post_x"]
    w = [wts[n] for n in names]
    n_state = CONV_WIDTH - 1
    return pl.pallas_call(
        functools.partial(_post_mix_kernel, fresh_sequence=fresh_sequence, conv_rows=conv_rows,
                          n_pieces=n_pieces),
        grid=grid,
        in_specs=[
            pl.BlockSpec((G, T, D), row),
            pl.BlockSpec((G, T, attn.shape[2]), row),
            pl.BlockSpec((G, T, D_CONV), row),
            pl.BlockSpec((G, CONV_HALO, D_CONV), halo_map),
        ] + mem_specs + [_full(a.shape) for a in w],
        out_specs=[pl.BlockSpec((G, T, D), row), pl.BlockSpec((G, n_state, D_CONV), per_b)],
        out_shape=[jax.ShapeDtypeStruct((B, S, D), F32), jax.ShapeDtypeStruct((B, n_state, D_CONV), F32)],
        scratch_shapes=[pltpu.VMEM((G, T + CONV_HALO, D_CONV), F32),
                        pltpu.VMEM((SUBLANES - 1, G, T + CONV_HALO - SUBLANES, D_CONV), F32)] + mem_scratch,
        compiler_params=_params("arbitrary" if mem_in_hbm else "parallel", "arbitrary"),
        name="post_mix_sample" if G > 1 else "post_mix_prompt",
    )(x, attn, u, halo, mk, mv, *w)


def _ffn_kernel(x_ref, state_ref, g_pre_ref, w_up_ref, w_fdw_ref, b_fdw_ref, w_down_ref, g_post_ref,
                y_ref, fstate_ref, gate_scr, *, n_pieces):
    G, T, D = x_ref.shape
    i = pl.program_id(1)
    n_hist = FFN_CONV_WIDTH - 1
    hist = slice(FFN_HALO - n_hist, FFN_HALO)

    @pl.when(i == 0)
    def _():
        gate_scr[:, hist, :] = state_ref[...]

    pieces = _pieces(G, T, n_pieces)
    xs = [_rows_of(x_ref, pc) for pc in pieces]
    ups = [_dot(_rms(x, g_pre_ref[...]).astype(BF16), w_up_ref[...]) for x in xs]
    acts = []
    for (gs, r0, nr), up in zip(pieces, ups):
        parts = []
        for gi, g in enumerate(gs):
            gate_scr[g, FFN_HALO + r0:FFN_HALO + r0 + nr, :] = up[gi * nr:(gi + 1) * nr, D_FF:]
            gate = jnp.broadcast_to(b_fdw_ref[...], (nr, D_FF))
            for k in range(FFN_CONV_WIDTH):
                off = FFN_HALO - n_hist + k + r0
                gate = gate + w_fdw_ref[k:k + 1, :] * gate_scr[g, off:off + nr, :]
            parts.append(gate)
        gate = jnp.concatenate(parts, axis=0) if len(parts) > 1 else parts[0]
        acts.append((gate * jax.nn.sigmoid(gate) * up[:, :D_FF]).astype(BF16))
    fs = [_dot(act, w_down_ref[...]) for act in acts]
    for pc, x, f in zip(pieces, xs, fs):
        _store_rows(y_ref, pc, x + _rms(f, g_post_ref[...]))

    new_state = gate_scr[:, T + FFN_HALO - n_hist:, :]
    fstate_ref[...] = new_state
    gate_scr[:, hist, :] = new_state


def _ffn(x, state, wts, *, G, T, n_pieces):
    B, S, D = x.shape
    row = lambda b, i: (b, i, 0)
    per_b = lambda b, i: (b, 0, 0)
    names = ["g_pre_ffn", "w_up", "w_fdw", "b_fdw", "w_down", "g_post_ffn"]
    w = [wts[n] for n in names]
    n_hist = FFN_CONV_WIDTH - 1
    return pl.pallas_call(
        functools.partial(_ffn_kernel, n_pieces=n_pieces),
        grid=(B // G, S // T),
        in_specs=[pl.BlockSpec((G, T, D), row), pl.BlockSpec((G, n_hist, D_FF), per_b)]
        + [pl.BlockSpec(a.shape, lambda b, i, n=a.ndim: (0,) * n, pipeline_mode=pl.Buffered(1)) for a in w],
        out_specs=[pl.BlockSpec((G, T, D), row), pl.BlockSpec((G, n_hist, D_FF), per_b)],
        out_shape=[jax.ShapeDtypeStruct((B, S, D), F32), jax.ShapeDtypeStruct((B, n_hist, D_FF), F32)],
        scratch_shapes=[pltpu.VMEM((G, T + FFN_HALO, D_FF), F32)],
        compiler_params=_params("parallel", "arbitrary"),
        name="ffn_sample" if G > 1 else "ffn_prompt",
    )(x, state, *w)


def _prepare_weights(w_in, g_q_a, w_q_b, g_kv_a, w_uk, w_uv, w_dw, b_dw, g_cln, b_cln, w_out,
                     w_xq, w_xo, w_up, w_fdw, b_fdw, w_down,
                     g_pre_mix, g_post_mix, g_pre_x, g_post_x, g_pre_ffn, g_post_ffn):
    assert w_in.shape[0] == 1, "one trunk layer"
    d_model = w_in.shape[1]
    zpad = lambda a, n: jnp.zeros(a.shape[:-1] + (n,), a.dtype)
    w_in0 = w_in[0]
    cuts = [Q_LORA, Q_LORA + KV_LORA, Q_LORA + KV_LORA + QK_ROPE, Q_LORA + KV_LORA + QK_ROPE + D_CONV]
    q_a, c_raw, pe_raw, glu_a, glu_b = jnp.split(w_in0, cuts, axis=-1)
    pe_blk = jnp.concatenate([zpad(pe_raw, ROPE_LANE0), pe_raw,
                              zpad(pe_raw, HEAD_BLOCK - ROPE_LANE0 - QK_ROPE)], axis=-1)
    w_in_p = jnp.concatenate([q_a, c_raw, pe_blk, glu_a, glu_b], axis=-1)
    assert w_in_p.shape == (d_model, D_IN_PAD)

    qk = QK_NOPE + QK_ROPE
    w_q = w_q_b[0].reshape(Q_LORA, N_HEADS, qk)
    w_q = jnp.concatenate([w_q, zpad(w_q, HEAD_BLOCK - qk)], axis=-1).reshape(Q_LORA, N_HEADS * HEAD_BLOCK)
    w_k = jnp.concatenate([w_uk[0], zpad(w_uk[0], HEAD_BLOCK - QK_NOPE)], axis=-1)
    w_k = w_k.reshape(KV_LORA, N_HEADS * HEAD_BLOCK)
    w_v = w_uv[0].reshape(KV_LORA, N_HEADS * V_HEAD)
    w_uk_t = jnp.transpose(w_uk[0], (1, 2, 0))
    eye = jnp.eye(N_HEADS, dtype=w_uv.dtype)
    w_uv_pad = jnp.einsum("lhv,hg->hlgv", w_uv[0], eye).reshape(N_HEADS, KV_LORA, N_HEADS * V_HEAD)

    bf = lambda a: a.astype(BF16)
    return {
        "w_in": bf(w_in_p), "g_q_a": g_q_a, "w_q": bf(w_q), "g_kv_a": g_kv_a,
        "w_k": bf(w_k), "w_v": bf(w_v), "w_uk_t": bf(w_uk_t), "w_uv_pad": bf(w_uv_pad),
        "w_dw": w_dw[0], "b_dw": b_dw, "g_cln": g_cln, "b_cln": b_cln, "w_out": bf(w_out[0]),
        "w_xq": bf(w_xq[0]), "w_xo": bf(w_xo[0]), "w_up": bf(w_up[0]), "w_fdw": w_fdw[0],
        "b_fdw": b_fdw, "w_down": bf(w_down[0]),
        "g_pre_mix": g_pre_mix, "g_post_mix": g_post_mix, "g_pre_x": g_pre_x, "g_post_x": g_post_x,
        "g_pre_ffn": g_pre_ffn, "g_post_ffn": g_post_ffn,
    }


TM_MIX = 512
TQ_FLASH = 512
T_POST = 512
T_FFN = 512
CONV_ROWS = 64
PIECES = 2
SAMPLE_SEQS = 16
SAMPLE_SEQS_POST = 4
PAGES_PER_CHUNK = 32
SUB_KEYS = 2048


def kernel(x_prompt, x_sample, cache_kv_latent, cache_k_rope, cache_mem_k, cache_mem_v, state_conv, state_ffn_conv, page_table, mem_prompt, w_in, g_q_a, w_q_b, g_kv_a, w_uk, w_uv, w_dw, b_dw, g_cln, b_cln, w_out, g_mem, w_xq, w_xk, w_xv, w_xo, w_up, w_fdw, b_fdw, w_down, g_pre_mix, g_post_mix, g_pre_x, g_post_x, g_pre_ffn, g_post_ffn):
    B, S, D = x_prompt.shape
    DB, DS, _ = x_sample.shape
    past_len = page_table.shape[1] * PAGE_SIZE
    wts = _prepare_weights(w_in, g_q_a, w_q_b, g_kv_a, w_uk, w_uv, w_dw, b_dw, g_cln, b_cln, w_out,
                           w_xq, w_xo, w_up, w_fdw, b_fdw, w_down,
                           g_pre_mix, g_post_mix, g_pre_x, g_post_x, g_pre_ffn, g_post_ffn)

    mk_p, mv_p, mkb_p, mvb_p = _mem_kv(mem_prompt, g_mem, w_xk[0].astype(BF16), w_xv[0].astype(BF16))
    tables_p = _rope_tables(S, 0, S)
    q, k, v, c_p, pe_p, u_p = _mix_in(x_prompt, tables_p, lambda b, i: (i, 0), wts, tm=TM_MIX, absorb=False)
    attn_p = _flash(q, k, v, tq=TQ_FLASH)
    halo_blocks = T_POST // CONV_HALO
    x2_p, cstate_p = _post_mix(
        x_prompt, attn_p, u_p, u_p, lambda b, i: (b, jnp.maximum(i * halo_blocks - 1, 0), 0),
        mkb_p, mvb_p, wts, G=1, T=T_POST, fresh_sequence=True, conv_rows=CONV_ROWS, n_pieces=PIECES)
    y_p, fstate_p = _ffn(x2_p, jnp.zeros((B, FFN_CONV_WIDTH - 1, D_FF), F32), wts, G=1, T=T_FFN, n_pieces=PIECES)

    rows = SAMPLE_SEQS * DS
    tables_s = _rope_tables(rows, past_len, DS)
    q_s, qlat_s, c_s, pe_s, u_s = _mix_in(
        x_sample.reshape(1, DB * DS, D), tables_s, lambda b, i: (0, 0), wts, tm=rows, absorb=True)
    qlat_s = qlat_s.reshape(DB, DS, N_HEADS, KV_LORA).transpose(0, 2, 1, 3).reshape(DB, N_HEADS * DS, KV_LORA)
    qpe_s = q_s.reshape(DB, DS, N_HEADS, HEAD_BLOCK)[..., ROPE_LANE0:ROPE_LANE0 + QK_ROPE]
    qpe_s = qpe_s.transpose(0, 2, 1, 3).reshape(DB, N_HEADS * DS, QK_ROPE)
    c_s = c_s.reshape(DB, DS, KV_LORA)
    pe_s = pe_s.reshape(DB, DS, QK_ROPE)
    attn_s = _paged_attention(page_table, qlat_s, qpe_s, c_s, pe_s, wts["w_uv_pad"],
                              cache_kv_latent.reshape(cache_kv_latent.shape[1:]),
                              jnp.swapaxes(cache_k_rope.reshape(cache_k_rope.shape[1:]), 1, 2),
                              pages_per_chunk=PAGES_PER_CHUNK, sub_keys=SUB_KEYS)
    n_state = CONV_WIDTH - 1
    halo_s = jnp.pad(state_conv[0], ((0, 0), (CONV_HALO - n_state, 0), (0, 0)))
    x2_s, cstate_s = _post_mix(
        x_sample, attn_s, u_s.reshape(DB, DS, D_CONV), halo_s, lambda b, i: (b, 0, 0),
        cache_mem_k.reshape(cache_mem_k.shape[1:]), cache_mem_v.reshape(cache_mem_v.shape[1:]), wts,
        G=SAMPLE_SEQS_POST, T=DS, fresh_sequence=False, conv_rows=DS, n_pieces=1)
    y_s, fstate_s = _ffn(x2_s, state_ffn_conv[0], wts, G=SAMPLE_SEQS, T=DS, n_pieces=1)

    mem_shape = (1, B, mem_prompt.shape[1], X_HEADS, X_HEAD_DIM)
    return (y_p, y_s, c_p[None], pe_p[None], cstate_p[None], fstate_p[None],
            mk_p.reshape(mem_shape), mv_p.reshape(mem_shape),
            c_s[None], pe_s[None], cstate_s[None], fstate_s[None])
```

```python
import functools
import math

import numpy as np
import jax
import jax.numpy as jnp
from jax import lax
from jax.experimental import pallas as pl
from jax.experimental.pallas import tpu as pltpu

F32 = jnp.float32
BF16 = jnp.bfloat16

PAGE_SIZE = 128
N_HEADS = 8
QK_NOPE = 64
QK_ROPE = 32
V_HEAD = 64
Q_LORA = 384
KV_LORA = 256
D_CONV = 512
CONV_WIDTH = 31
X_HEADS = 4
X_HEAD_DIM = 256
D_FF = 2816
FFN_CONV_WIDTH = 3
ROPE_THETA = 10000.0
EPS = 1e-6
NEG_INF = -1e30
LOG2E = math.log2(math.e)
ATTN_SCALE = (QK_NOPE + QK_ROPE) ** -0.5 * LOG2E
X_SCALE = X_HEAD_DIM ** -0.5 * LOG2E

LANES = 128
SUBLANES = 8
VMEM_LIMIT = 56 * 1024 * 1024

HEAD_BLOCK = LANES
ROPE_LANE0 = QK_NOPE
ROPE_HALF = QK_ROPE // 2
CONV_HALO = 32
FFN_HALO = SUBLANES
ROPE_TABLE_ROWS = 256
ONES_ROWS = 16
N_SLOTS = 4

_C_Q = 0
_C_C = _C_Q + Q_LORA
_C_PE = _C_C + KV_LORA
_C_GA = _C_PE + HEAD_BLOCK
_C_GB = _C_GA + D_CONV
D_IN_PAD = _C_GB + D_CONV


def _params(*sem):
    return pltpu.CompilerParams(dimension_semantics=sem, vmem_limit_bytes=VMEM_LIMIT)


def _full(shape):
    n = len(shape)
    return pl.BlockSpec(shape, lambda *_: (0,) * n)


def _rms(x, g):
    return x * lax.rsqrt(jnp.mean(x * x, axis=-1, keepdims=True) + EPS) * g


def _dot(a, b):
    return jnp.dot(a, b, preferred_element_type=F32)


def _dot_nt(a, b):
    return lax.dot_general(a, b, (((1,), (1,)), ((), ())), preferred_element_type=F32)


def _rope_table_kernel(inv_ref, cos_ref, sin_a_ref, sin_b_ref, *, pos0, period):
    n = cos_ref.shape[0]
    row = pl.program_id(0) * n + lax.broadcasted_iota(jnp.int32, (n, LANES), 0)
    lane = lax.broadcasted_iota(jnp.int32, (n, LANES), 1)
    pos = (pos0 + row % period).astype(F32)
    ang = pos * inv_ref[...]
    cos, sin = jnp.cos(ang), jnp.sin(ang)
    first = (lane >= ROPE_LANE0) & (lane < ROPE_LANE0 + ROPE_HALF)
    second = (lane >= ROPE_LANE0 + ROPE_HALF) & (lane < ROPE_LANE0 + QK_ROPE)
    cos_ref[...] = jnp.where(first | second, cos, 1.0)
    sin_a_ref[...] = jnp.where(first, -sin, 0.0)
    sin_b_ref[...] = jnp.where(second, sin, 0.0)


def _rope_tables(n_rows, pos0, period):
    inv = np.zeros((1, LANES), np.float32)
    freq = ROPE_THETA ** (-np.arange(ROPE_HALF, dtype=np.float64) / ROPE_HALF)
    inv[0, ROPE_LANE0:ROPE_LANE0 + ROPE_HALF] = freq
    inv[0, ROPE_LANE0 + ROPE_HALF:ROPE_LANE0 + QK_ROPE] = freq
    out = jax.ShapeDtypeStruct((n_rows, LANES), F32)
    rows = min(n_rows, ROPE_TABLE_ROWS)
    blk = pl.BlockSpec((rows, LANES), lambda i: (i, 0))
    return pl.pallas_call(
        functools.partial(_rope_table_kernel, pos0=pos0, period=period),
        grid=(n_rows // rows,),
        in_specs=[_full(inv.shape)],
        out_specs=(blk, blk, blk),
        out_shape=(out, out, out),
        compiler_params=_params("parallel"),
        name="rope_tables",
    )(jnp.asarray(inv))


def _rope_block(x, cos, sin_a, sin_b):
    left = pltpu.roll(x, LANES - ROPE_HALF, 1)
    right = pltpu.roll(x, ROPE_HALF, 1)
    return x * cos + left * sin_a + right * sin_b


def _mix_in_kernel(x_ref, cos_ref, sin_a_ref, sin_b_ref, g_pre_ref, w_in_ref, g_q_ref, w_q_ref,
                   g_kv_ref, w_a_ref, w_b_ref, *out_refs, absorb):
    x = x_ref[0]
    h = _rms(x, g_pre_ref[...]).astype(BF16)
    proj = _dot(h, w_in_ref[...])
    cos, sin_a, sin_b = cos_ref[...], sin_a_ref[...], sin_b_ref[...]

    qn = _rms(proj[:, _C_Q:_C_C], g_q_ref[...]).astype(BF16)
    q = _dot(qn, w_q_ref[...])
    c_new = _rms(proj[:, _C_C:_C_PE], g_kv_ref[...])
    pe_blk = _rope_block(proj[:, _C_PE:_C_GA], cos, sin_a, sin_b)
    u = proj[:, _C_GA:_C_GB] * jax.nn.sigmoid(proj[:, _C_GB:])

    if absorb:
        q_ref, qlat_ref, c_ref, pe_ref, u_ref = out_refs
    else:
        q_ref, k_ref, v_ref, c_ref, pe_ref, u_ref = out_refs
        cb = c_new.astype(BF16)
        k = _dot(cb, w_a_ref[...])
        v_ref[0] = _dot(cb, w_b_ref[...]).astype(BF16)

    for hd in range(N_HEADS):
        blk = slice(hd * HEAD_BLOCK, (hd + 1) * HEAD_BLOCK)
        q_h = (_rope_block(q[:, blk], cos, sin_a, sin_b) * ATTN_SCALE).astype(BF16)
        q_ref[0, :, blk] = q_h
        if absorb:
            qlat_ref[0, :, hd * KV_LORA:(hd + 1) * KV_LORA] = _dot(
                q_h[:, :QK_NOPE], w_a_ref[hd]).astype(BF16)
        else:
            k_ref[0, :, blk] = (k[:, blk] + pe_blk).astype(BF16)

    c_ref[0] = c_new
    pe_ref[0] = pe_blk[:, ROPE_LANE0:ROPE_LANE0 + QK_ROPE]
    u_ref[0] = u


def _mix_in(x, tables, table_map, wts, *, tm, absorb):
    B, S, D = x.shape
    grid = (B, S // tm)
    row = lambda b, i: (b, i, 0)
    tab = pl.BlockSpec((tm, LANES), table_map)
    w_a, w_b = (wts["w_uk_t"], wts["w_uv_pad"]) if absorb else (wts["w_k"], wts["w_v"])
    ins = [x, *tables, wts["g_pre_mix"], wts["w_in"], wts["g_q_a"], wts["w_q"], wts["g_kv_a"], w_a, w_b]
    in_specs = [pl.BlockSpec((1, tm, D), row), tab, tab, tab] + [_full(a.shape) for a in ins[4:]]
    hq = N_HEADS * HEAD_BLOCK
    out = [((B, S, hq), BF16)]
    if absorb:
        out += [((B, S, N_HEADS * KV_LORA), BF16)]
    else:
        out += [((B, S, hq), BF16), ((B, S, N_HEADS * V_HEAD), BF16)]
    out += [((B, S, KV_LORA), F32), ((B, S, QK_ROPE), F32), ((B, S, D_CONV), F32)]
    return pl.pallas_call(
        functools.partial(_mix_in_kernel, absorb=absorb),
        grid=grid,
        in_specs=in_specs,
        out_specs=[pl.BlockSpec((1, tm, s[-1]), row) for s, _ in out],
        out_shape=[jax.ShapeDtypeStruct(s, d) for s, d in out],
        compiler_params=_params("parallel", "arbitrary"),
        name="mix_in_sample" if absorb else "mix_in_prompt",
    )(*ins)


def _transpose_bf16(x):
    return x.astype(F32).T.astype(BF16)


def _flash_kernel(q_ref, k_ref, v_ref, o_ref, vt_scr, s0_scr, m_scr, acc_scr, *, tq):
    qi = pl.program_id(2)
    n_kv = vt_scr.shape[0]

    @pl.when(qi == 0)
    def _():
        ones = jnp.ones((ONES_ROWS, tq), BF16)

        def put(t, _):
            rows = pl.ds(pl.multiple_of(t * tq, tq), tq)
            v_t = v_ref[0, rows, :].astype(F32).T
            vt_scr[t, 0] = jnp.concatenate([v_t[:V_HEAD].astype(BF16), ones], axis=0)
            vt_scr[t, 1] = jnp.concatenate([v_t[V_HEAD:].astype(BF16), ones], axis=0)
            return 0
        lax.fori_loop(0, n_kv, put, 0)

    key = lax.broadcasted_iota(jnp.int32, (tq, tq), 0)
    qry = lax.broadcasted_iota(jnp.int32, (tq, tq), 1)
    blks = [slice(j * HEAD_BLOCK, (j + 1) * HEAD_BLOCK) for j in range(2)]
    q_t = [_transpose_bf16(q_ref[0, :, blk]) for blk in blks]

    def scores(t, j):
        rows = pl.ds(pl.multiple_of(t * tq, tq), tq)
        return _dot(k_ref[0, rows, blks[j]], q_t[j])

    def absorb_tile(t, j, s, masked):
        if masked:
            s = jnp.where(key <= qry, s, NEG_INF)
        m = m_scr[j]
        m_new = jnp.maximum(m, jnp.max(s, axis=0, keepdims=True))
        alpha = jnp.exp2(m - m_new)
        p = jnp.exp2(s - m_new)
        m_scr[j] = m_new
        acc_scr[j] = alpha * acc_scr[j] + _dot(vt_scr[t, j], p.astype(BF16))

    m_scr[...] = jnp.full(m_scr.shape, NEG_INF, F32)
    acc_scr[...] = jnp.zeros(acc_scr.shape, F32)

    s0_scr[...] = scores(0, 0)

    def full_tile(t):
        s1 = scores(t, 1)
        absorb_tile(t, 0, s0_scr[...], False)
        s0_scr[...] = scores(t + 1, 0)
        absorb_tile(t, 1, s1, False)

    def four_tiles(tq4, _):
        for d in range(4):
            full_tile(4 * tq4 + d)
        return 0

    lax.fori_loop(0, qi // 4, four_tiles, 0)
    done = (qi // 4) * 4

    @pl.when(qi - done >= 2)
    def _():
        full_tile(done)
        full_tile(done + 1)

    @pl.when(qi % 2 == 1)
    def _():
        full_tile(qi - 1)

    s1 = scores(qi, 1)
    absorb_tile(qi, 0, s0_scr[...], True)
    absorb_tile(qi, 1, s1, True)
    heads = [acc_scr[j, :V_HEAD, :] / acc_scr[j, V_HEAD:V_HEAD + 1, :] for j in range(2)]
    o_ref[0] = jnp.concatenate(heads, axis=0).T.astype(BF16)


def _flash(q, k, v, *, tq):
    B, S, _ = q.shape
    pairs = N_HEADS // 2
    return pl.pallas_call(
        functools.partial(_flash_kernel, tq=tq),
        grid=(B, pairs, S // tq),
        in_specs=[
            pl.BlockSpec((1, tq, 2 * HEAD_BLOCK), lambda b, p, i: (b, i, p)),
            pl.BlockSpec((1, S, 2 * HEAD_BLOCK), lambda b, p, i: (b, 0, p)),
            pl.BlockSpec((1, S, 2 * V_HEAD), lambda b, p, i: (b, 0, p)),
        ],
        out_specs=pl.BlockSpec((1, tq, 2 * V_HEAD), lambda b, p, i: (b, i, p)),
        out_shape=jax.ShapeDtypeStruct((B, S, N_HEADS * V_HEAD), BF16),
        scratch_shapes=[
            pltpu.VMEM((S // tq, 2, V_HEAD + ONES_ROWS, tq), BF16),
            pltpu.VMEM((tq, tq), F32),
            pltpu.VMEM((2, 1, tq), F32),
            pltpu.VMEM((2, V_HEAD + ONES_ROWS, tq), F32),
        ],
        compiler_params=_params("parallel", "parallel", "arbitrary"),
        name="flash_prompt",
    )(q, k, v)


def _paged_kernel(pt_ref, qlat_ref, qpe_ref, cnew_ref, penew_ref, wuv_ref, ckv_hbm, kr_hbm, o_ref,
                  cbuf, pbuf, sem, cb16, *, pages_per_chunk, sub_keys):
    b = pl.program_id(0)
    n_chunks = pl.num_programs(0) * N_SLOTS
    n_rows = qlat_ref.shape[1]
    chunk_keys = pages_per_chunk * PAGE_SIZE

    def page_copies(chunk, sl, j):
        page = pt_ref[chunk * pages_per_chunk + j]
        keys = pl.ds(j * PAGE_SIZE, PAGE_SIZE)
        return (pltpu.make_async_copy(ckv_hbm.at[page], cbuf.at[sl, keys], sem.at[sl]),
                pltpu.make_async_copy(kr_hbm.at[page], pbuf.at[sl, :, keys], sem.at[sl]))

    def start_chunk(chunk, sl):
        for j in range(pages_per_chunk):
            for cp in page_copies(chunk, sl, j):
                cp.start()

    def wait_chunk(chunk, sl):
        for j in range(pages_per_chunk):
            for cp in page_copies(chunk, sl, j):
                cp.wait()

    @pl.when(b == 0)
    def _():
        for k in range(N_SLOTS - 1):
            start_chunk(k, k)

    qlat = qlat_ref[0]
    qpe = qpe_ref[0]

    def update(carry, s, vals):
        m, l, acc = carry
        m_new = jnp.maximum(m, jnp.max(s, axis=-1, keepdims=True))
        alpha = jnp.exp2(m - m_new)
        p = jnp.exp2(s - m_new)
        l = alpha * l + jnp.sum(p, axis=-1, keepdims=True)
        acc = alpha * acc + _dot(p.astype(BF16), vals)
        return m_new, l, acc

    n_sub = chunk_keys // sub_keys
    carry = (jnp.full((n_rows, 1), NEG_INF, F32), jnp.zeros((n_rows, 1), F32), jnp.zeros((n_rows, KV_LORA), F32))
    for sl in range(N_SLOTS):
        chunk = b * N_SLOTS + sl
        wait_chunk(chunk, sl)
        nxt = chunk + (N_SLOTS - 1)
        nxt = jnp.where(nxt >= n_chunks, nxt - n_chunks, nxt)
        start_chunk(nxt, (sl + N_SLOTS - 1) % N_SLOTS)

        def scores(i):
            keys = slice(i * sub_keys, (i + 1) * sub_keys)
            cb16[keys, :] = cbuf[sl, keys, :].astype(BF16)
            return _dot_nt(qlat, cb16[keys, :]) + _dot(qpe, pbuf[sl, :, keys].astype(BF16))

        s = scores(0)
        for i in range(n_sub):
            s_next = scores(i + 1) if i + 1 < n_sub else None
            carry = update(carry, s, cb16[i * sub_keys:(i + 1) * sub_keys, :])
            s = s_next

    n_new = cnew_ref.shape[1]
    cb = cnew_ref[0].astype(BF16)
    pb = penew_ref[0].astype(BF16)
    s = _dot_nt(qlat, cb) + _dot_nt(qpe, pb)
    q_s = lax.broadcasted_iota(jnp.int32, (n_rows, n_new), 0) % n_new
    t = lax.broadcasted_iota(jnp.int32, (n_rows, n_new), 1)
    s = jnp.where(q_s >= t, s, NEG_INF)
    _, l, acc = update(carry, s, cb)
    o_lat = acc / l
    out = jnp.zeros((n_new, N_HEADS * V_HEAD), F32)
    for hd in range(N_HEADS):
        out = out + _dot(o_lat[hd * n_new:(hd + 1) * n_new].astype(BF16), wuv_ref[hd])
    o_ref[0] = out

    @pl.when(b == pl.num_programs(0) - 1)
    def _():
        for k in range(N_SLOTS - 1):
            wait_chunk(k, k)


def _paged_attention(page_table, qlat, qpe, c_new, pe_new, w_uv_pad, cache_c, cache_pe_t,
                     *, pages_per_chunk, sub_keys):
    DB, n_rows, _ = qlat.shape
    n_new = c_new.shape[1]
    n_pages = page_table.shape[1]
    assert n_pages == N_SLOTS * pages_per_chunk
    chunk_keys = pages_per_chunk * PAGE_SIZE
    per_b = lambda b, pt: (b, 0, 0)
    grid_spec = pltpu.PrefetchScalarGridSpec(
        num_scalar_prefetch=1,
        grid=(DB,),
        in_specs=[
            pl.BlockSpec((1, n_rows, KV_LORA), per_b),
            pl.BlockSpec((1, n_rows, QK_ROPE), per_b),
            pl.BlockSpec((1, n_new, KV_LORA), per_b),
            pl.BlockSpec((1, n_new, QK_ROPE), per_b),
            pl.BlockSpec(w_uv_pad.shape, lambda b, pt: (0, 0, 0)),
            pl.BlockSpec(memory_space=pl.ANY),
            pl.BlockSpec(memory_space=pl.ANY),
        ],
        out_specs=pl.BlockSpec((1, n_new, N_HEADS * V_HEAD), per_b),
        scratch_shapes=[
            pltpu.VMEM((N_SLOTS, chunk_keys, KV_LORA), F32),
            pltpu.VMEM((N_SLOTS, QK_ROPE, chunk_keys), F32),
            pltpu.SemaphoreType.DMA((N_SLOTS,)),
            pltpu.VMEM((chunk_keys, KV_LORA), BF16),
        ],
    )
    return pl.pallas_call(
        functools.partial(_paged_kernel, pages_per_chunk=pages_per_chunk, sub_keys=sub_keys),
        grid_spec=grid_spec,
        out_shape=jax.ShapeDtypeStruct((DB, n_new, N_HEADS * V_HEAD), F32),
        compiler_params=_params("arbitrary"),
        name="paged_sample",
    )(page_table.reshape(-1), qlat, qpe, c_new, pe_new, w_uv_pad, cache_c, cache_pe_t)


def _mem_kv_kernel(mem_ref, g_ref, w_k_ref, w_v_ref, k_ref, v_ref, kb_ref, vb_ref):
    m = _rms(mem_ref[0], g_ref[...]).astype(BF16)
    k = _dot(m, w_k_ref[...])
    v = _dot(m, w_v_ref[...])
    k_ref[0], v_ref[0] = k, v
    kb_ref[0], vb_ref[0] = k.astype(BF16), v.astype(BF16)


def _mem_kv(mem, g_mem, w_xk, w_xv):
    B, N, D = mem.shape
    blk = pl.BlockSpec((1, N, D), lambda b: (b, 0, 0))
    cols = X_HEADS * X_HEAD_DIM
    out_blk = pl.BlockSpec((1, N, cols), lambda b: (b, 0, 0))
    out = jax.ShapeDtypeStruct((B, N, cols), F32)
    out_b = jax.ShapeDtypeStruct((B, N, cols), BF16)
    return pl.pallas_call(
        _mem_kv_kernel,
        grid=(B,),
        in_specs=[blk, _full(g_mem.shape), _full(w_xk.shape), _full(w_xv.shape)],
        out_specs=[out_blk] * 4,
        out_shape=[out, out, out_b, out_b],
        compiler_params=_params("parallel"),
        name="mem_kv",
    )(mem, g_mem, w_xk, w_xv)


def _pieces(G, T, n_pieces):
    if G == 1:
        return [([0], p * (T // n_pieces), T // n_pieces) for p in range(n_pieces)]
    per = G // n_pieces
    return [(list(range(p * per, (p + 1) * per)), 0, T) for p in range(n_pieces)]


def _rows_of(ref, piece):
    gs, r0, nr = piece
    parts = [ref[g, r0:r0 + nr, :] for g in gs]
    return jnp.concatenate(parts, axis=0) if len(parts) > 1 else parts[0]


def _store_rows(ref, piece, val):
    gs, r0, nr = piece
    for gi, g in enumerate(gs):
        ref[g, r0:r0 + nr, :] = val[gi * nr:(gi + 1) * nr]


def _post_mix_kernel(x_ref, attn_ref, u_ref, halo_ref, mk_ref, mv_ref, w_dw_ref, b_dw_ref, g_cln_ref,
                     b_cln_ref, w_out_ref, g_post_mix_ref, g_pre_x_ref, w_xq_ref, w_xo_ref,
                     g_post_x_ref, x2_ref, cstate_ref, full_scr, shift_scr, *mem_scr, fresh_sequence, conv_rows,
                     n_pieces):
    G, T, D = x_ref.shape
    i = pl.program_id(1)
    if mem_scr:
        mk_buf, mv_buf, mem_sem = mem_scr
        step, n_steps = pl.program_id(0), pl.num_programs(0)
        slot = step % 2

        def mem_copies(stp, sl):
            return [pltpu.make_async_copy(src.at[stp * G + g, :, hd, :], dst.at[sl, g, hd], mem_sem.at[sl])
                    for g in range(G) for hd in range(X_HEADS) for src, dst in ((mk_ref, mk_buf), (mv_ref, mv_buf))]

        @pl.when(step == 0)
        def _():
            for cp in mem_copies(0, 0):
                cp.start()

        nxt = jnp.where(step + 1 == n_steps, 0, step + 1)
        for cp in mem_copies(nxt, 1 - slot):
            cp.start()

        def mem_tiles(g, hd, cols):
            return mk_buf[slot, g, hd].astype(BF16), mv_buf[slot, g, hd].astype(BF16)
    else:
        def mem_tiles(g, hd, cols):
            return mk_ref[g, :, cols], mv_ref[g, :, cols]
    halo = halo_ref[...]
    if fresh_sequence:
        halo = jnp.where(i == 0, 0.0, halo)
    full_scr[:, :CONV_HALO, :] = halo
    full_scr[:, CONV_HALO:, :] = u_ref[...]
    cstate_ref[...] = full_scr[:, T + CONV_HALO - (CONV_WIDTH - 1):, :]
    n_shift = shift_scr.shape[2]
    for j in range(1, SUBLANES):
        shift_scr[j - 1] = full_scr[:, j:j + n_shift, :]

    pieces = _pieces(G, T, n_pieces)

    def conv_group(piece):
        gs, r0, nr = piece
        base = CONV_HALO - (CONV_WIDTH - 1)
        parts = []
        for g in gs:
            for c0 in range(r0, r0 + nr, conv_rows):
                acc = jnp.broadcast_to(b_dw_ref[...], (conv_rows, D_CONV))
                for k in range(CONV_WIDTH):
                    a, j = divmod(base + k, SUBLANES)
                    rows = slice(c0 + a * SUBLANES, c0 + a * SUBLANES + conv_rows)
                    src = full_scr[g, rows, :] if j == 0 else shift_scr[j - 1, g, rows, :]
                    acc = acc + w_dw_ref[k:k + 1, :] * src
                parts.append(acc)
        cv = jnp.concatenate(parts, axis=0) if len(parts) > 1 else parts[0]
        xc = cv - jnp.mean(cv, axis=-1, keepdims=True)
        var = jnp.mean(xc * xc, axis=-1, keepdims=True)
        cv = xc * lax.rsqrt(var + EPS) * g_cln_ref[...] + b_cln_ref[...]
        return (cv * jax.nn.sigmoid(cv)).astype(BF16)

    d_attn = attn_ref.shape[2]
    cvs = [conv_group(pc) for pc in pieces]
    mixes = [_dot(_rows_of(attn_ref, pc).astype(BF16), w_out_ref[:d_attn, :]) + _dot(cv, w_out_ref[d_attn:, :])
             for pc, cv in zip(pieces, cvs)]
    x1s = [_rows_of(x_ref, pc) + _rms(mix, g_post_mix_ref[...]) for pc, mix in zip(pieces, mixes)]

    qs = [_dot(_rms(x1, g_pre_x_ref[...]).astype(BF16), w_xq_ref[...]) * X_SCALE for x1 in x1s]
    if mem_scr:
        for cp in mem_copies(step, slot):
            cp.wait()
    os_ = []
    for (gs, r0, nr), q in zip(pieces, qs):
        heads = [(gi, g, hd, slice(hd * X_HEAD_DIM, (hd + 1) * X_HEAD_DIM))
                 for gi, g in enumerate(gs) for hd in range(X_HEADS)]
        mems = [mem_tiles(g, hd, cols) for gi, g, hd, cols in heads]
        scores = [_dot_nt(q[gi * nr:(gi + 1) * nr, cols].astype(BF16), mk)
                  for (gi, g, hd, cols), (mk, _) in zip(heads, mems)]
        outs = []
        for (_, mv), s in zip(mems, scores):
            p = jnp.exp2(s - jnp.max(s, axis=-1, keepdims=True))
            l = jnp.sum(p, axis=-1, keepdims=True)
            outs.append(_dot(p.astype(BF16), mv) / l)
        o_rows = [jnp.concatenate(outs[gi * X_HEADS:(gi + 1) * X_HEADS], axis=-1) for gi in range(len(gs))]
        os_.append(jnp.concatenate(o_rows, axis=0) if len(o_rows) > 1 else o_rows[0])
    xas = [_dot(o.astype(BF16), w_xo_ref[...]) for o in os_]
    for pc, x1, xa in zip(pieces, x1s, xas):
        _store_rows(x2_ref, pc, x1 + _rms(xa, g_post_x_ref[...]))

    if mem_scr:
        @pl.when(step == n_steps - 1)
        def _():
            for cp in mem_copies(nxt, 1 - slot):
                cp.wait()


def _post_mix(x, attn, u, halo, halo_map, mk, mv, wts, *, G, T, fresh_sequence, conv_rows, n_pieces):
    B, S, D = x.shape
    grid = (B // G, S // T)
    mem_in_hbm = mk.ndim == 4
    if mem_in_hbm:
        mem_specs = [pl.BlockSpec(memory_space=pl.ANY)] * 2
        tiles = (2, G, X_HEADS, mk.shape[1], X_HEAD_DIM)
        mem_scratch = [pltpu.VMEM(tiles, F32), pltpu.VMEM(tiles, F32), pltpu.SemaphoreType.DMA((2,))]
    else:
        mem_specs = [pl.BlockSpec((G,) + mk.shape[1:], lambda b, i: (b, 0, 0))] * 2
        mem_scratch = []
    row = lambda b, i: (b, i, 0)
    per_b = lambda b, i: (b, 0, 0)
    names = ["w_dw", "b_dw", "g_cln", "b_cln", "w_out", "g_post_mix", "g_pre_x", "w_xq", "w_xo", "g_post_x"]
    w = [wts[n] for n in names]
    n_state = CONV_WIDTH - 1
    return pl.pallas_call(
        functools.partial(_post_mix_kernel, fresh_sequence=fresh_sequence, conv_rows=conv_rows,
                          n_pieces=n_pieces),
        grid=grid,
        in_specs=[
            pl.BlockSpec((G, T, D), row),
            pl.BlockSpec((G, T, attn.shape[2]), row),
            pl.BlockSpec((G, T, D_CONV), row),
            pl.BlockSpec((G, CONV_HALO, D_CONV), halo_map),
        ] + mem_specs + [_full(a.shape) for a in w],
        out_specs=[pl.BlockSpec((G, T, D), row), pl.BlockSpec((G, n_state, D_CONV), per_b)],
        out_shape=[jax.ShapeDtypeStruct((B, S, D), F32), jax.ShapeDtypeStruct((B, n_state, D_CONV), F32)],
        scratch_shapes=[pltpu.VMEM((G, T + CONV_HALO, D_CONV), F32),
                        pltpu.VMEM((SUBLANES - 1, G, T + CONV_HALO - SUBLANES, D_CONV), F32)] + mem_scratch,
        compiler_params=_params("arbitrary" if mem_in_hbm else "parallel", "arbitrary"),
        name="post_mix_sample" if G > 1 else "post_mix_prompt",
    )(x, attn, u, halo, mk, mv, *w)


def _ffn_kernel(x_ref, state_ref, g_pre_ref, w_up_ref, w_fdw_ref, b_fdw_ref, w_down_ref, g_post_ref,
                y_ref, fstate_ref, gate_scr, *, n_pieces):
    G, T, D = x_ref.shape
    i = pl.program_id(1)
    n_hist = FFN_CONV_WIDTH - 1
    hist = slice(FFN_HALO - n_hist, FFN_HALO)

    @pl.when(i == 0)
    def _():
        gate_scr[:, hist, :] = state_ref[...]

    pieces = _pieces(G, T, n_pieces)
    xs = [_rows_of(x_ref, pc) for pc in pieces]
    ups = [_dot(_rms(x, g_pre_ref[...]).astype(BF16), w_up_ref[...]) for x in xs]
    acts = []
    for (gs, r0, nr), up in zip(pieces, ups):
        parts = []
        for gi, g in enumerate(gs):
            gate_scr[g, FFN_HALO + r0:FFN_HALO + r0 + nr, :] = up[gi * nr:(gi + 1) * nr, D_FF:]
            gate = jnp.broadcast_to(b_fdw_ref[...], (nr, D_FF))
            for k in range(FFN_CONV_WIDTH):
                off = FFN_HALO - n_hist + k + r0
                gate = gate + w_fdw_ref[k:k + 1, :] * gate_scr[g, off:off + nr, :]
            parts.append(gate)
        gate = jnp.concatenate(parts, axis=0) if len(parts) > 1 else parts[0]
        acts.append((gate * jax.nn.sigmoid(gate) * up[:, :D_FF]).astype(BF16))
    fs = [_dot(act, w_down_ref[...]) for act in acts]
    for pc, x, f in zip(pieces, xs, fs):
        _store_rows(y_ref, pc, x + _rms(f, g_post_ref[...]))

    new_state = gate_scr[:, T + FFN_HALO - n_hist:, :]
    fstate_ref[...] = new_state
    gate_scr[:, hist, :] = new_state


def _ffn(x, state, wts, *, G, T, n_pieces):
    B, S, D = x.shape
    row = lambda b, i: (b, i, 0)
    per_b = lambda b, i: (b, 0, 0)
    names = ["g_pre_ffn", "w_up", "w_fdw", "b_fdw", "w_down", "g_post_ffn"]
    w = [wts[n] for n in names]
    n_hist = FFN_CONV_WIDTH - 1
    return pl.pallas_call(
        functools.partial(_ffn_kernel, n_pieces=n_pieces),
        grid=(B // G, S // T),
        in_specs=[pl.BlockSpec((G, T, D), row), pl.BlockSpec((G, n_hist, D_FF), per_b)]
        + [pl.BlockSpec(a.shape, lambda b, i, n=a.ndim: (0,) * n, pipeline_mode=pl.Buffered(1)) for a in w],
        out_specs=[pl.BlockSpec((G, T, D), row), pl.BlockSpec((G, n_hist, D_FF), per_b)],
        out_shape=[jax.ShapeDtypeStruct((B, S, D), F32), jax.ShapeDtypeStruct((B, n_hist, D_FF), F32)],
        scratch_shapes=[pltpu.VMEM((G, T + FFN_HALO, D_FF), F32)],
        compiler_params=_params("parallel", "arbitrary"),
        name="ffn_sample" if G > 1 else "ffn_prompt",
    )(x, state, *w)


def _prepare_weights(w_in, g_q_a, w_q_b, g_kv_a, w_uk, w_uv, w_dw, b_dw, g_cln, b_cln, w_out,
                     w_xq, w_xo, w_up, w_fdw, b_fdw, w_down,
                     g_pre_mix, g_post_mix, g_pre_x, g_post_x, g_pre_ffn, g_post_ffn):
    assert w_in.shape[0] == 1, "one trunk layer"
    d_model = w_in.shape[1]
    zpad = lambda a, n: jnp.zeros(a.shape[:-1] + (n,), a.dtype)
    w_in0 = w_in[0]
    cuts = [Q_LORA, Q_LORA + KV_LORA, Q_LORA + KV_LORA + QK_ROPE, Q_LORA + KV_LORA + QK_ROPE + D_CONV]
    q_a, c_raw, pe_raw, glu_a, glu_b = jnp.split(w_in0, cuts, axis=-1)
    pe_blk = jnp.concatenate([zpad(pe_raw, ROPE_LANE0), pe_raw,
                              zpad(pe_raw, HEAD_BLOCK - ROPE_LANE0 - QK_ROPE)], axis=-1)
    w_in_p = jnp.concatenate([q_a, c_raw, pe_blk, glu_a, glu_b], axis=-1)
    assert w_in_p.shape == (d_model, D_IN_PAD)

    qk = QK_NOPE + QK_ROPE
    w_q = w_q_b[0].reshape(Q_LORA, N_HEADS, qk)
    w_q = jnp.concatenate([w_q, zpad(w_q, HEAD_BLOCK - qk)], axis=-1).reshape(Q_LORA, N_HEADS * HEAD_BLOCK)
    w_k = jnp.concatenate([w_uk[0], zpad(w_uk[0], HEAD_BLOCK - QK_NOPE)], axis=-1)
    w_k = w_k.reshape(KV_LORA, N_HEADS * HEAD_BLOCK)
    w_v = w_uv[0].reshape(KV_LORA, N_HEADS * V_HEAD)
    w_uk_t = jnp.transpose(w_uk[0], (1, 2, 0))
    eye = jnp.eye(N_HEADS, dtype=w_uv.dtype)
    w_uv_pad = jnp.einsum("lhv,hg->hlgv", w_uv[0], eye).reshape(N_HEADS, KV_LORA, N_HEADS * V_HEAD)

    bf = lambda a: a.astype(BF16)
    return {
        "w_in": bf(w_in_p), "g_q_a": g_q_a, "w_q": bf(w_q), "g_kv_a": g_kv_a,
        "w_k": bf(w_k), "w_v": bf(w_v), "w_uk_t": bf(w_uk_t), "w_uv_pad": bf(w_uv_pad),
        "w_dw": w_dw[0], "b_dw": b_dw, "g_cln": g_cln, "b_cln": b_cln, "w_out": bf(w_out[0]),
        "w_xq": bf(w_xq[0]), "w_xo": bf(w_xo[0]), "w_up": bf(w_up[0]), "w_fdw": w_fdw[0],
        "b_fdw": b_fdw, "w_down": bf(w_down[0]),
        "g_pre_mix": g_pre_mix, "g_post_mix": g_post_mix, "g_pre_x": g_pre_x, "g_post_x": g_post_x,
        "g_pre_ffn": g_pre_ffn, "g_post_ffn": g_post_ffn,
    }


TM_MIX = 512
TQ_FLASH = 512
T_POST = 512
T_FFN = 512
CONV_ROWS = 64
PIECES = 2
SAMPLE_SEQS = 16
SAMPLE_SEQS_POST = 4
PAGES_PER_CHUNK = 32
SUB_KEYS = 2048


def kernel(x_prompt, x_sample, cache_kv_latent, cache_k_rope, cache_mem_k, cache_mem_v, state_conv, state_ffn_conv, page_table, mem_prompt, w_in, g_q_a, w_q_b, g_kv_a, w_uk, w_uv, w_dw, b_dw, g_cln, b_cln, w_out, g_mem, w_xq, w_xk, w_xv, w_xo, w_up, w_fdw, b_fdw, w_down, g_pre_mix, g_post_mix, g_pre_x, g_post_x, g_pre_ffn, g_post_ffn):
    B, S, D = x_prompt.shape
    DB, DS, _ = x_sample.shape
    past_len = page_table.shape[1] * PAGE_SIZE
    wts = _prepare_weights(w_in, g_q_a, w_q_b, g_kv_a, w_uk, w_uv, w_dw, b_dw, g_cln, b_cln, w_out,
                           w_xq, w_xo, w_up, w_fdw, b_fdw, w_down,
                           g_pre_mix, g_post_mix, g_pre_x, g_post_x, g_pre_ffn, g_post_ffn)

    mk_p, mv_p, mkb_p, mvb_p = _mem_kv(mem_prompt, g_mem, w_xk[0].astype(BF16), w_xv[0].astype(BF16))
    tables_p = _rope_tables(S, 0, S)
    q, k, v, c_p, pe_p, u_p = _mix_in(x_prompt, tables_p, lambda b, i: (i, 0), wts, tm=TM_MIX, absorb=False)
    attn_p = _flash(q, k, v, tq=TQ_FLASH)
    halo_blocks = T_POST // CONV_HALO
    x2_p, cstate_p = _post_mix(
        x_prompt, attn_p, u_p, u_p, lambda b, i: (b, jnp.maximum(i * halo_blocks - 1, 0), 0),
        mkb_p, mvb_p, wts, G=1, T=T_POST, fresh_sequence=True, conv_rows=CONV_ROWS, n_pieces=PIECES)
    y_p, fstate_p = _ffn(x2_p, jnp.zeros((B, FFN_CONV_WIDTH - 1, D_FF), F32), wts, G=1, T=T_FFN, n_pieces=PIECES)

    rows = SAMPLE_SEQS * DS
    tables_s = _rope_tables(rows, past_len, DS)
    q_s, qlat_s, c_s, pe_s, u_s = _mix_in(
        x_sample.reshape(1, DB * DS, D), tables_s, lambda b, i: (0, 0), wts, tm=rows, absorb=True)
    qlat_s = qlat_s.reshape(DB, DS, N_HEADS, KV_LORA).transpose(0, 2, 1, 3).reshape(DB, N_HEADS * DS, KV_LORA)
    qpe_s = q_s.reshape(DB, DS, N_HEADS, HEAD_BLOCK)[..., ROPE_LANE0:ROPE_LANE0 + QK_ROPE]
    qpe_s = qpe_s.transpose(0, 2, 1, 3).reshape(DB, N_HEADS * DS, QK_ROPE)
    c_s = c_s.reshape(DB, DS, KV_LORA)
    pe_s = pe_s.reshape(DB, DS, QK_ROPE)
    attn_s = _paged_attention(page_table, qlat_s, qpe_s, c_s, pe_s, wts["w_uv_pad"],
                              cache_kv_latent.reshape(cache_kv_latent.shape[1:]),
                              jnp.swapaxes(cache_k_rope.reshape(cache_k_rope.shape[1:]), 1, 2),
                              pages_per_chunk=PAGES_PER_CHUNK, sub_keys=SUB_KEYS)
    n_state = CONV_WIDTH - 1
    halo_s = jnp.pad(state_conv[0], ((0, 0), (CONV_HALO - n_state, 0), (0, 0)))
    x2_s, cstate_s = _post_mix(
        x_sample, attn_s, u_s.reshape(DB, DS, D_CONV), halo_s, lambda b, i: (b, 0, 0),
        cache_mem_k.reshape(cache_mem_k.shape[1:]), cache_mem_v.reshape(cache_mem_v.shape[1:]), wts,
        G=SAMPLE_SEQS_POST, T=DS, fresh_sequence=False, conv_rows=DS, n_pieces=1)
    y_s, fstate_s = _ffn(x2_s, state_ffn_conv[0], wts, G=SAMPLE_SEQS, T=DS, n_pieces=1)

    mem_shape = (1, B, mem_prompt.shape[1], X_HEADS, X_HEAD_DIM)
    return (y_p, y_s, c_p[None], pe_p[None], cstate_p[None], fstate_p[None],
            mk_p.reshape(mem_shape), mv_p.reshape(mem_shape),
            c_s[None], pe_s[None], cstate_s[None], fstate_s[None])
```

```python
import functools
import math

import numpy as np
import jax
import jax.numpy as jnp
from jax import lax
from jax.experimental import pallas as pl
from jax.experimental.pallas import tpu as pltpu

F32 = jnp.float32
BF16 = jnp.bfloat16

PAGE_SIZE = 128
N_HEADS = 8
QK_NOPE = 64
QK_ROPE = 32
V_HEAD = 64
Q_LORA = 384
KV_LORA = 256
D_CONV = 512
CONV_WIDTH = 31
X_HEADS = 4
X_HEAD_DIM = 256
D_FF = 2816
FFN_CONV_WIDTH = 3
ROPE_THETA = 10000.0
EPS = 1e-6
NEG_INF = -1e30
LOG2E = math.log2(math.e)
ATTN_SCALE = (QK_NOPE + QK_ROPE) ** -0.5 * LOG2E
X_SCALE = X_HEAD_DIM ** -0.5 * LOG2E

LANES = 128
SUBLANES = 8
VMEM_LIMIT = 56 * 1024 * 1024

HEAD_BLOCK = LANES
ROPE_LANE0 = QK_NOPE
ROPE_HALF = QK_ROPE // 2
CONV_HALO = 32
FFN_HALO = SUBLANES
ROPE_TABLE_ROWS = 256
ONES_ROWS = 16
N_SLOTS = 4

_C_Q = 0
_C_C = _C_Q + Q_LORA
_C_PE = _C_C + KV_LORA
_C_GA = _C_PE + HEAD_BLOCK
_C_GB = _C_GA + D_CONV
D_IN_PAD = _C_GB + D_CONV


def _params(*sem):
    return pltpu.CompilerParams(dimension_semantics=sem, vmem_limit_bytes=VMEM_LIMIT)


def _full(shape):
    n = len(shape)
    return pl.BlockSpec(shape, lambda *_: (0,) * n)


def _rms(x, g):
    return x * lax.rsqrt(jnp.mean(x * x, axis=-1, keepdims=True) + EPS) * g


def _dot(a, b):
    return jnp.dot(a, b, preferred_element_type=F32)


def _dot_nt(a, b):
    return lax.dot_general(a, b, (((1,), (1,)), ((), ())), preferred_element_type=F32)


def _rope_table_kernel(inv_ref, cos_ref, sin_a_ref, sin_b_ref, *, pos0, period):
    n = cos_ref.shape[0]
    row = pl.program_id(0) * n + lax.broadcasted_iota(jnp.int32, (n, LANES), 0)
    lane = lax.broadcasted_iota(jnp.int32, (n, LANES), 1)
    pos = (pos0 + row % period).astype(F32)
    ang = pos * inv_ref[...]
    cos, sin = jnp.cos(ang), jnp.sin(ang)
    first = (lane >= ROPE_LANE0) & (lane < ROPE_LANE0 + ROPE_HALF)
    second = (lane >= ROPE_LANE0 + ROPE_HALF) & (lane < ROPE_LANE0 + QK_ROPE)
    cos_ref[...] = jnp.where(first | second, cos, 1.0)
    sin_a_ref[...] = jnp.where(first, -sin, 0.0)
    sin_b_ref[...] = jnp.where(second, sin, 0.0)


def _rope_tables(n_rows, pos0, period):
    inv = np.zeros((1, LANES), np.float32)
    freq = ROPE_THETA ** (-np.arange(ROPE_HALF, dtype=np.float64) / ROPE_HALF)
    inv[0, ROPE_LANE0:ROPE_LANE0 + ROPE_HALF] = freq
    inv[0, ROPE_LANE0 + ROPE_HALF:ROPE_LANE0 + QK_ROPE] = freq
    out = jax.ShapeDtypeStruct((n_rows, LANES), F32)
    rows = min(n_rows, ROPE_TABLE_ROWS)
    blk = pl.BlockSpec((rows, LANES), lambda i: (i, 0))
    return pl.pallas_call(
        functools.partial(_rope_table_kernel, pos0=pos0, period=period),
        grid=(n_rows // rows,),
        in_specs=[_full(inv.shape)],
        out_specs=(blk, blk, blk),
        out_shape=(out, out, out),
        compiler_params=_params("parallel"),
        name="rope_tables",
    )(jnp.asarray(inv))


def _rope_block(x, cos, sin_a, sin_b):
    left = pltpu.roll(x, LANES - ROPE_HALF, 1)
    right = pltpu.roll(x, ROPE_HALF, 1)
    return x * cos + left * sin_a + right * sin_b


def _mix_in_kernel(x_ref, cos_ref, sin_a_ref, sin_b_ref, g_pre_ref, w_in_ref, g_q_ref, w_q_ref,
                   g_kv_ref, w_a_ref, w_b_ref, *out_refs, absorb):
    x = x_ref[0]
    h = _rms(x, g_pre_ref[...]).astype(BF16)
    proj = _dot(h, w_in_ref[...])
    cos, sin_a, sin_b = cos_ref[...], sin_a_ref[...], sin_b_ref[...]

    qn = _rms(proj[:, _C_Q:_C_C], g_q_ref[...]).astype(BF16)
    q = _dot(qn, w_q_ref[...])
    c_new = _rms(proj[:, _C_C:_C_PE], g_kv_ref[...])
    pe_blk = _rope_block(proj[:, _C_PE:_C_GA], cos, sin_a, sin_b)
    u = proj[:, _C_GA:_C_GB] * jax.nn.sigmoid(proj[:, _C_GB:])

    if absorb:
        q_ref, qlat_ref, c_ref, pe_ref, u_ref = out_refs
    else:
        q_ref, k_ref, v_ref, c_ref, pe_ref, u_ref = out_refs
        cb = c_new.astype(BF16)
        k = _dot(cb, w_a_ref[...])
        v_ref[0] = _dot(cb, w_b_ref[...]).astype(BF16)

    for hd in range(N_HEADS):
        blk = slice(hd * HEAD_BLOCK, (hd + 1) * HEAD_BLOCK)
        q_h = (_rope_block(q[:, blk], cos, sin_a, sin_b) * ATTN_SCALE).astype(BF16)
        q_ref[0, :, blk] = q_h
        if absorb:
            qlat_ref[0, :, hd * KV_LORA:(hd + 1) * KV_LORA] = _dot(
                q_h[:, :QK_NOPE], w_a_ref[hd]).astype(BF16)
        else:
            k_ref[0, :, blk] = (k[:, blk] + pe_blk).astype(BF16)

    c_ref[0] = c_new
    pe_ref[0] = pe_blk[:, ROPE_LANE0:ROPE_LANE0 + QK_ROPE]
    u_ref[0] = u


def _mix_in(x, tables, table_map, wts, *, tm, absorb):
    B, S, D = x.shape
    grid = (B, S // tm)
    row = lambda b, i: (b, i, 0)
    tab = pl.BlockSpec((tm, LANES), table_map)
    w_a, w_b = (wts["w_uk_t"], wts["w_uv_pad"]) if absorb else (wts["w_k"], wts["w_v"])
    ins = [x, *tables, wts["g_pre_mix"], wts["w_in"], wts["g_q_a"], wts["w_q"], wts["g_kv_a"], w_a, w_b]
    in_specs = [pl.BlockSpec((1, tm, D), row), tab, tab, tab] + [_full(a.shape) for a in ins[4:]]
    hq = N_HEADS * HEAD_BLOCK
    out = [((B, S, hq), BF16)]
    if absorb:
        out += [((B, S, N_HEADS * KV_LORA), BF16)]
    else:
        out += [((B, S, hq), BF16), ((B, S, N_HEADS * V_HEAD), BF16)]
    out += [((B, S, KV_LORA), F32), ((B, S, QK_ROPE), F32), ((B, S, D_CONV), F32)]
    return pl.pallas_call(
        functools.partial(_mix_in_kernel, absorb=absorb),
        grid=grid,
        in_specs=in_specs,
        out_specs=[pl.BlockSpec((1, tm, s[-1]), row) for s, _ in out],
        out_shape=[jax.ShapeDtypeStruct(s, d) for s, d in out],
        compiler_params=_params("parallel", "arbitrary"),
        name="mix_in_sample" if absorb else "mix_in_prompt",
    )(*ins)


def _transpose_bf16(x):
    return x.astype(F32).T.astype(BF16)


def _flash_kernel(q_ref, k_ref, v_ref, o_ref, vt_scr, s0_scr, m_scr, acc_scr, *, tq):
    qi = pl.program_id(2)
    n_kv = vt_scr.shape[0]

    @pl.when(qi == 0)
    def _():
        ones = jnp.ones((ONES_ROWS, tq), BF16)

        def put(t, _):
            rows = pl.ds(pl.multiple_of(t * tq, tq), tq)
            v_t = v_ref[0, rows, :].astype(F32).T
            vt_scr[t, 0] = jnp.concatenate([v_t[:V_HEAD].astype(BF16), ones], axis=0)
            vt_scr[t, 1] = jnp.concatenate([v_t[V_HEAD:].astype(BF16), ones], axis=0)
            return 0
        lax.fori_loop(0, n_kv, put, 0)

    key = lax.broadcasted_iota(jnp.int32, (tq, tq), 0)
    qry = lax.broadcasted_iota(jnp.int32, (tq, tq), 1)
    blks = [slice(j * HEAD_BLOCK, (j + 1) * HEAD_BLOCK) for j in range(2)]
    q_t = [_transpose_bf16(q_ref[0, :, blk]) for blk in blks]

    def scores(t, j):
        rows = pl.ds(pl.multiple_of(t * tq, tq), tq)
        return _dot(k_ref[0, rows, blks[j]], q_t[j])

    def absorb_tile(t, j, s, masked):
        if masked:
            s = jnp.where(key <= qry, s, NEG_INF)
        m = m_scr[j]
        m_new = jnp.maximum(m, jnp.max(s, axis=0, keepdims=True))
        alpha = jnp.exp2(m - m_new)
        p = jnp.exp2(s - m_new)
        m_scr[j] = m_new
        acc_scr[j] = alpha * acc_scr[j] + _dot(vt_scr[t, j], p.astype(BF16))

    m_scr[...] = jnp.full(m_scr.shape, NEG_INF, F32)
    acc_scr[...] = jnp.zeros(acc_scr.shape, F32)

    s0_scr[...] = scores(0, 0)

    def full_tile(t):
        s1 = scores(t, 1)
        absorb_tile(t, 0, s0_scr[...], False)
        s0_scr[...] = scores(t + 1, 0)
        absorb_tile(t, 1, s1, False)

    def unrolled(tb, _):
        for d in range(FLASH_UNROLL):
            full_tile(FLASH_UNROLL * tb + d)
        return 0

    lax.fori_loop(0, qi // FLASH_UNROLL, unrolled, 0)
    done = (qi // FLASH_UNROLL) * FLASH_UNROLL
    width = FLASH_UNROLL // 2
    while width >= 1:
        take = ((qi - done) // width) % 2 == 1

        @pl.when(take)
        def _(done=done, width=width):
            for d in range(width):
                full_tile(done + d)

        done = done + jnp.where(take, width, 0)
        width //= 2

    s1 = scores(qi, 1)
    absorb_tile(qi, 0, s0_scr[...], True)
    absorb_tile(qi, 1, s1, True)
    heads = [acc_scr[j, :V_HEAD, :] / acc_scr[j, V_HEAD:V_HEAD + 1, :] for j in range(2)]
    o_ref[0] = jnp.concatenate(heads, axis=0).T.astype(BF16)


def _flash(q, k, v, *, tq):
    B, S, _ = q.shape
    pairs = N_HEADS // 2
    return pl.pallas_call(
        functools.partial(_flash_kernel, tq=tq),
        grid=(B, pairs, S // tq),
        in_specs=[
            pl.BlockSpec((1, tq, 2 * HEAD_BLOCK), lambda b, p, i: (b, i, p)),
            pl.BlockSpec((1, S, 2 * HEAD_BLOCK), lambda b, p, i: (b, 0, p)),
            pl.BlockSpec((1, S, 2 * V_HEAD), lambda b, p, i: (b, 0, p)),
        ],
        out_specs=pl.BlockSpec((1, tq, 2 * V_HEAD), lambda b, p, i: (b, i, p)),
        out_shape=jax.ShapeDtypeStruct((B, S, N_HEADS * V_HEAD), BF16),
        scratch_shapes=[
            pltpu.VMEM((S // tq, 2, V_HEAD + ONES_ROWS, tq), BF16),
            pltpu.VMEM((tq, tq), F32),
            pltpu.VMEM((2, 1, tq), F32),
            pltpu.VMEM((2, V_HEAD + ONES_ROWS, tq), F32),
        ],
        compiler_params=_params("parallel", "parallel", "arbitrary"),
        name="flash_prompt",
    )(q, k, v)


def _paged_kernel(pt_ref, qlat_ref, qpe_ref, cnew_ref, penew_ref, wuv_ref, ckv_hbm, kr_hbm, o_ref,
                  cbuf, pbuf, sem, cb16, *, pages_per_chunk, sub_keys):
    b = pl.program_id(0)
    n_chunks = pl.num_programs(0) * N_SLOTS
    n_rows = qlat_ref.shape[1]
    chunk_keys = pages_per_chunk * PAGE_SIZE

    def page_copies(chunk, sl, j):
        page = pt_ref[chunk * pages_per_chunk + j]
        keys = pl.ds(j * PAGE_SIZE, PAGE_SIZE)
        return (pltpu.make_async_copy(ckv_hbm.at[page], cbuf.at[sl, keys], sem.at[sl]),
                pltpu.make_async_copy(kr_hbm.at[page], pbuf.at[sl, :, keys], sem.at[sl]))

    def start_chunk(chunk, sl):
        for j in range(pages_per_chunk):
            for cp in page_copies(chunk, sl, j):
                cp.start()

    def wait_chunk(chunk, sl):
        for j in range(pages_per_chunk):
            for cp in page_copies(chunk, sl, j):
                cp.wait()

    @pl.when(b == 0)
    def _():
        for k in range(N_SLOTS - 1):
            start_chunk(k, k)

    qlat = qlat_ref[0]
    qpe = qpe_ref[0]

    def update(carry, s, vals):
        m, l, acc = carry
        m_new = jnp.maximum(m, jnp.max(s, axis=-1, keepdims=True))
        alpha = jnp.exp2(m - m_new)
        p = jnp.exp2(s - m_new)
        l = alpha * l + jnp.sum(p, axis=-1, keepdims=True)
        acc = alpha * acc + _dot(p.astype(BF16), vals)
        return m_new, l, acc

    n_sub = chunk_keys // sub_keys
    carry = (jnp.full((n_rows, 1), NEG_INF, F32), jnp.zeros((n_rows, 1), F32), jnp.zeros((n_rows, KV_LORA), F32))
    for sl in range(N_SLOTS):
        chunk = b * N_SLOTS + sl
        wait_chunk(chunk, sl)
        nxt = chunk + (N_SLOTS - 1)
        nxt = jnp.where(nxt >= n_chunks, nxt - n_chunks, nxt)
        start_chunk(nxt, (sl + N_SLOTS - 1) % N_SLOTS)

        def scores(i):
            keys = slice(i * sub_keys, (i + 1) * sub_keys)
            cb16[keys, :] = cbuf[sl, keys, :].astype(BF16)
            return _dot_nt(qlat, cb16[keys, :]) + _dot(qpe, pbuf[sl, :, keys].astype(BF16))

        s = scores(0)
        for i in range(n_sub):
            s_next = scores(i + 1) if i + 1 < n_sub else None
            carry = update(carry, s, cb16[i * sub_keys:(i + 1) * sub_keys, :])
            s = s_next

    n_new = cnew_ref.shape[1]
    cb = cnew_ref[0].astype(BF16)
    pb = penew_ref[0].astype(BF16)
    s = _dot_nt(qlat, cb) + _dot_nt(qpe, pb)
    q_s = lax.broadcasted_iota(jnp.int32, (n_rows, n_new), 0) % n_new
    t = lax.broadcasted_iota(jnp.int32, (n_rows, n_new), 1)
    s = jnp.where(q_s >= t, s, NEG_INF)
    _, l, acc = update(carry, s, cb)
    o_lat = acc / l
    out = jnp.zeros((n_new, N_HEADS * V_HEAD), F32)
    for hd in range(N_HEADS):
        out = out + _dot(o_lat[hd * n_new:(hd + 1) * n_new].astype(BF16), wuv_ref[hd])
    o_ref[0] = out

    @pl.when(b == pl.num_programs(0) - 1)
    def _():
        for k in range(N_SLOTS - 1):
            wait_chunk(k, k)


def _paged_attention(page_table, qlat, qpe, c_new, pe_new, w_uv_pad, cache_c, cache_pe_t,
                     *, pages_per_chunk, sub_keys):
    DB, n_rows, _ = qlat.shape
    n_new = c_new.shape[1]
    n_pages = page_table.shape[1]
    assert n_pages == N_SLOTS * pages_per_chunk
    chunk_keys = pages_per_chunk * PAGE_SIZE
    per_b = lambda b, pt: (b, 0, 0)
    grid_spec = pltpu.PrefetchScalarGridSpec(
        num_scalar_prefetch=1,
        grid=(DB,),
        in_specs=[
            pl.BlockSpec((1, n_rows, KV_LORA), per_b),
            pl.BlockSpec((1, n_rows, QK_ROPE), per_b),
            pl.BlockSpec((1, n_new, KV_LORA), per_b),
            pl.BlockSpec((1, n_new, QK_ROPE), per_b),
            pl.BlockSpec(w_uv_pad.shape, lambda b, pt: (0, 0, 0)),
            pl.BlockSpec(memory_space=pl.ANY),
            pl.BlockSpec(memory_space=pl.ANY),
        ],
        out_specs=pl.BlockSpec((1, n_new, N_HEADS * V_HEAD), per_b),
        scratch_shapes=[
            pltpu.VMEM((N_SLOTS, chunk_keys, KV_LORA), F32),
            pltpu.VMEM((N_SLOTS, QK_ROPE, chunk_keys), F32),
            pltpu.SemaphoreType.DMA((N_SLOTS,)),
            pltpu.VMEM((chunk_keys, KV_LORA), BF16),
        ],
    )
    return pl.pallas_call(
        functools.partial(_paged_kernel, pages_per_chunk=pages_per_chunk, sub_keys=sub_keys),
        grid_spec=grid_spec,
        out_shape=jax.ShapeDtypeStruct((DB, n_new, N_HEADS * V_HEAD), F32),
        compiler_params=_params("arbitrary"),
        name="paged_sample",
    )(page_table.reshape(-1), qlat, qpe, c_new, pe_new, w_uv_pad, cache_c, cache_pe_t)


def _mem_kv_kernel(mem_ref, g_ref, w_k_ref, w_v_ref, k_ref, v_ref, kb_ref, vb_ref):
    m = _rms(mem_ref[0], g_ref[...]).astype(BF16)
    k = _dot(m, w_k_ref[...])
    v = _dot(m, w_v_ref[...])
    k_ref[0], v_ref[0] = k, v
    kb_ref[0], vb_ref[0] = k.astype(BF16), v.astype(BF16)


def _mem_kv(mem, g_mem, w_xk, w_xv):
    B, N, D = mem.shape
    blk = pl.BlockSpec((1, N, D), lambda b: (b, 0, 0))
    cols = X_HEADS * X_HEAD_DIM
    out_blk = pl.BlockSpec((1, N, cols), lambda b: (b, 0, 0))
    out = jax.ShapeDtypeStruct((B, N, cols), F32)
    out_b = jax.ShapeDtypeStruct((B, N, cols), BF16)
    return pl.pallas_call(
        _mem_kv_kernel,
        grid=(B,),
        in_specs=[blk, _full(g_mem.shape), _full(w_xk.shape), _full(w_xv.shape)],
        out_specs=[out_blk] * 4,
        out_shape=[out, out, out_b, out_b],
        compiler_params=_params("parallel"),
        name="mem_kv",
    )(mem, g_mem, w_xk, w_xv)


def _pieces(G, T, n_pieces):
    if G == 1:
        return [([0], p * (T // n_pieces), T // n_pieces) for p in range(n_pieces)]
    per = G // n_pieces
    return [(list(range(p * per, (p + 1) * per)), 0, T) for p in range(n_pieces)]


def _rows_of(ref, piece):
    gs, r0, nr = piece
    parts = [ref[g, r0:r0 + nr, :] for g in gs]
    return jnp.concatenate(parts, axis=0) if len(parts) > 1 else parts[0]


def _store_rows(ref, piece, val):
    gs, r0, nr = piece
    for gi, g in enumerate(gs):
        ref[g, r0:r0 + nr, :] = val[gi * nr:(gi + 1) * nr]


def _post_mix_kernel(x_ref, attn_ref, u_ref, halo_ref, mk_ref, mv_ref, w_dw_ref, b_dw_ref, g_cln_ref,
                     b_cln_ref, w_out_ref, g_post_mix_ref, g_pre_x_ref, w_xq_ref, w_xo_ref,
                     g_post_x_ref, x2_ref, cstate_ref, full_scr, shift_scr, *mem_scr, fresh_sequence, conv_rows,
                     n_pieces):
    G, T, D = x_ref.shape
    i = pl.program_id(1)
    if mem_scr:
        mk_buf, mv_buf, mem_sem = mem_scr
        step, n_steps = pl.program_id(0), pl.num_programs(0)
        slot = step % 2

        def mem_copies(stp, sl):
            return [pltpu.make_async_copy(src.at[stp * G + g, :, hd, :], dst.at[sl, g, hd], mem_sem.at[sl])
                    for g in range(G) for hd in range(X_HEADS) for src, dst in ((mk_ref, mk_buf), (mv_ref, mv_buf))]

        @pl.when(step == 0)
        def _():
            for cp in mem_copies(0, 0):
                cp.start()

        nxt = jnp.where(step + 1 == n_steps, 0, step + 1)
        for cp in mem_copies(nxt, 1 - slot):
            cp.start()

        def mem_tiles(g, hd, cols):
            return mk_buf[slot, g, hd].astype(BF16), mv_buf[slot, g, hd].astype(BF16)
    else:
        def mem_tiles(g, hd, cols):
            return mk_ref[g, :, cols], mv_ref[g, :, cols]
    halo = halo_ref[...]
    if fresh_sequence:
        halo = jnp.where(i == 0, 0.0, halo)
    full_scr[:, :CONV_HALO, :] = halo
    full_scr[:, CONV_HALO:, :] = u_ref[...]
    cstate_ref[...] = full_scr[:, T + CONV_HALO - (CONV_WIDTH - 1):, :]
    n_shift = shift_scr.shape[2]
    for j in range(1, SUBLANES):
        shift_scr[j - 1] = full_scr[:, j:j + n_shift, :]

    pieces = _pieces(G, T, n_pieces)

    def conv_group(piece):
        gs, r0, nr = piece
        base = CONV_HALO - (CONV_WIDTH - 1)
        parts = []
        for g in gs:
            for c0 in range(r0, r0 + nr, conv_rows):
                acc = jnp.broadcast_to(b_dw_ref[...], (conv_rows, D_CONV))
                for k in range(CONV_WIDTH):
                    a, j = divmod(base + k, SUBLANES)
                    rows = slice(c0 + a * SUBLANES, c0 + a * SUBLANES + conv_rows)
                    src = full_scr[g, rows, :] if j == 0 else shift_scr[j - 1, g, rows, :]
                    acc = acc + w_dw_ref[k:k + 1, :] * src
                parts.append(acc)
        cv = jnp.concatenate(parts, axis=0) if len(parts) > 1 else parts[0]
        xc = cv - jnp.mean(cv, axis=-1, keepdims=True)
        var = jnp.mean(xc * xc, axis=-1, keepdims=True)
        cv = xc * lax.rsqrt(var + EPS) * g_cln_ref[...] + b_cln_ref[...]
        return (cv * jax.nn.sigmoid(cv)).astype(BF16)

    d_attn = attn_ref.shape[2]
    cvs = [conv_group(pc) for pc in pieces]
    mixes = [_dot(_rows_of(attn_ref, pc).astype(BF16), w_out_ref[:d_attn, :]) + _dot(cv, w_out_ref[d_attn:, :])
             for pc, cv in zip(pieces, cvs)]
    x1s = [_rows_of(x_ref, pc) + _rms(mix, g_post_mix_ref[...]) for pc, mix in zip(pieces, mixes)]

    qs = [_dot(_rms(x1, g_pre_x_ref[...]).astype(BF16), w_xq_ref[...]) * X_SCALE for x1 in x1s]
    if mem_scr:
        for cp in mem_copies(step, slot):
            cp.wait()
    os_ = []
    for (gs, r0, nr), q in zip(pieces, qs):
        heads = [(gi, g, hd, slice(hd * X_HEAD_DIM, (hd + 1) * X_HEAD_DIM))
                 for gi, g in enumerate(gs) for hd in range(X_HEADS)]
        mems = [mem_tiles(g, hd, cols) for gi, g, hd, cols in heads]
        scores = [_dot_nt(q[gi * nr:(gi + 1) * nr, cols].astype(BF16), mk)
                  for (gi, g, hd, cols), (mk, _) in zip(heads, mems)]
        outs = []
        for (_, mv), s in zip(mems, scores):
            p = jnp.exp2(s - jnp.max(s, axis=-1, keepdims=True))
            l = jnp.sum(p, axis=-1, keepdims=True)
            outs.append(_dot(p.astype(BF16), mv) / l)
        o_rows = [jnp.concatenate(outs[gi * X_HEADS:(gi + 1) * X_HEADS], axis=-1) for gi in range(len(gs))]
        os_.append(jnp.concatenate(o_rows, axis=0) if len(o_rows) > 1 else o_rows[0])
    xas = [_dot(o.astype(BF16), w_xo_ref[...]) for o in os_]
    for pc, x1, xa in zip(pieces, x1s, xas):
        _store_rows(x2_ref, pc, x1 + _rms(xa, g_post_x_ref[...]))

    if mem_scr:
        @pl.when(step == n_steps - 1)
        def _():
            for cp in mem_copies(nxt, 1 - slot):
                cp.wait()


def _post_mix(x, attn, u, halo, halo_map, mk, mv, wts, *, G, T, fresh_sequence, conv_rows, n_pieces):
    B, S, D = x.shape
    grid = (B // G, S // T)
    mem_in_hbm = mk.ndim == 4
    if mem_in_hbm:
        mem_specs = [pl.BlockSpec(memory_space=pl.ANY)] * 2
        tiles = (2, G, X_HEADS, mk.shape[1], X_HEAD_DIM)
        mem_scratch = [pltpu.VMEM(tiles, F32), pltpu.VMEM(tiles, F32), pltpu.SemaphoreType.DMA((2,))]
    else:
        mem_specs = [pl.BlockSpec((G,) + mk.shape[1:], lambda b, i: (b, 0, 0))] * 2
        mem_scratch = []
    row = lambda b, i: (b, i, 0)
    per_b = lambda b, i: (b, 0, 0)
    names = ["w_dw", "b_dw", "g_cln", "b_cln", "w_out", "g_post_mix", "g_pre_x", "w_xq", "w_xo", "g_post_x"]
    w = [wts[n] for n in names]
    n_state = CONV_WIDTH - 1
    return pl.pallas_call(
        functools.partial(_post_mix_kernel, fresh_sequence=fresh_sequence, conv_rows=conv_rows,
                          n_pieces=n_pieces),
        grid=grid,
        in_specs=[
            pl.BlockSpec((G, T, D), row),
            pl.BlockSpec((G, T, attn.shape[2]), row),
            pl.BlockSpec((G, T, D_CONV), row),
            pl.BlockSpec((G, CONV_HALO, D_CONV), halo_map),
        ] + mem_specs + [_full(a.shape) for a in w],
        out_specs=[pl.BlockSpec((G, T, D), row), pl.BlockSpec((G, n_state, D_CONV), per_b)],
        out_shape=[jax.ShapeDtypeStruct((B, S, D), F32), jax.ShapeDtypeStruct((B, n_state, D_CONV), F32)],
        scratch_shapes=[pltpu.VMEM((G, T + CONV_HALO, D_CONV), F32),
                        pltpu.VMEM((SUBLANES - 1, G, T + CONV_HALO - SUBLANES, D_CONV), F32)] + mem_scratch,
        compiler_params=_params("arbitrary" if mem_in_hbm else "parallel", "arbitrary"),
        name="post_mix_sample" if G > 1 else "post_mix_prompt",
    )(x, attn, u, halo, mk, mv, *w)


def _ffn_kernel(x_ref, state_ref, g_pre_ref, w_up_ref, w_fdw_ref, b_fdw_ref, w_down_ref, g_post_ref,
                y_ref, fstate_ref, gate_scr, *, n_pieces):
    G, T, D = x_ref.shape
    i = pl.program_id(1)
    n_hist = FFN_CONV_WIDTH - 1
    hist = slice(FFN_HALO - n_hist, FFN_HALO)

    @pl.when(i == 0)
    def _():
        gate_scr[:, hist, :] = state_ref[...]

    pieces = _pieces(G, T, n_pieces)
    xs = [_rows_of(x_ref, pc) for pc in pieces]
    ups = [_dot(_rms(x, g_pre_ref[...]).astype(BF16), w_up_ref[...]) for x in xs]
    acts = []
    for (gs, r0, nr), up in zip(pieces, ups):
        parts = []
        for gi, g in enumerate(gs):
            gate_scr[g, FFN_HALO + r0:FFN_HALO + r0 + nr, :] = up[gi * nr:(gi + 1) * nr, D_FF:]
            gate = jnp.broadcast_to(b_fdw_ref[...], (nr, D_FF))
            for k in range(FFN_CONV_WIDTH):
                off = FFN_HALO - n_hist + k + r0
                gate = gate + w_fdw_ref[k:k + 1, :] * gate_scr[g, off:off + nr, :]
            parts.append(gate)
        gate = jnp.concatenate(parts, axis=0) if len(parts) > 1 else parts[0]
        acts.append((gate * jax.nn.sigmoid(gate) * up[:, :D_FF]).astype(BF16))
    fs = [_dot(act, w_down_ref[...]) for act in acts]
    for pc, x, f in zip(pieces, xs, fs):
        _store_rows(y_ref, pc, x + _rms(f, g_post_ref[...]))

    new_state = gate_scr[:, T + FFN_HALO - n_hist:, :]
    fstate_ref[...] = new_state
    gate_scr[:, hist, :] = new_state


def _ffn(x, state, wts, *, G, T, n_pieces):
    B, S, D = x.shape
    row = lambda b, i: (b, i, 0)
    per_b = lambda b, i: (b, 0, 0)
    names = ["g_pre_ffn", "w_up", "w_fdw", "b_fdw", "w_down", "g_post_ffn"]
    w = [wts[n] for n in names]
    n_hist = FFN_CONV_WIDTH - 1
    return pl.pallas_call(
        functools.partial(_ffn_kernel, n_pieces=n_pieces),
        grid=(B // G, S // T),
        in_specs=[pl.BlockSpec((G, T, D), row), pl.BlockSpec((G, n_hist, D_FF), per_b)]
        + [pl.BlockSpec(a.shape, lambda b, i, n=a.ndim: (0,) * n, pipeline_mode=pl.Buffered(1)) for a in w],
        out_specs=[pl.BlockSpec((G, T, D), row), pl.BlockSpec((G, n_hist, D_FF), per_b)],
        out_shape=[jax.ShapeDtypeStruct((B, S, D), F32), jax.ShapeDtypeStruct((B, n_hist, D_FF), F32)],
        scratch_shapes=[pltpu.VMEM((G, T + FFN_HALO, D_FF), F32)],
        compiler_params=_params("parallel", "arbitrary"),
        name="ffn_sample" if G > 1 else "ffn_prompt",
    )(x, state, *w)


def _prepare_weights(w_in, g_q_a, w_q_b, g_kv_a, w_uk, w_uv, w_dw, b_dw, g_cln, b_cln, w_out,
                     w_xq, w_xo, w_up, w_fdw, b_fdw, w_down,
                     g_pre_mix, g_post_mix, g_pre_x, g_post_x, g_pre_ffn, g_post_ffn):
    assert w_in.shape[0] == 1, "one trunk layer"
    d_model = w_in.shape[1]
    zpad = lambda a, n: jnp.zeros(a.shape[:-1] + (n,), a.dtype)
    w_in0 = w_in[0]
    cuts = [Q_LORA, Q_LORA + KV_LORA, Q_LORA + KV_LORA + QK_ROPE, Q_LORA + KV_LORA + QK_ROPE + D_CONV]
    q_a, c_raw, pe_raw, glu_a, glu_b = jnp.split(w_in0, cuts, axis=-1)
    pe_blk = jnp.concatenate([zpad(pe_raw, ROPE_LANE0), pe_raw,
                              zpad(pe_raw, HEAD_BLOCK - ROPE_LANE0 - QK_ROPE)], axis=-1)
    w_in_p = jnp.concatenate([q_a, c_raw, pe_blk, glu_a, glu_b], axis=-1)
    assert w_in_p.shape == (d_model, D_IN_PAD)

    qk = QK_NOPE + QK_ROPE
    w_q = w_q_b[0].reshape(Q_LORA, N_HEADS, qk)
    w_q = jnp.concatenate([w_q, zpad(w_q, HEAD_BLOCK - qk)], axis=-1).reshape(Q_LORA, N_HEADS * HEAD_BLOCK)
    w_k = jnp.concatenate([w_uk[0], zpad(w_uk[0], HEAD_BLOCK - QK_NOPE)], axis=-1)
    w_k = w_k.reshape(KV_LORA, N_HEADS * HEAD_BLOCK)
    w_v = w_uv[0].reshape(KV_LORA, N_HEADS * V_HEAD)
    w_uk_t = jnp.transpose(w_uk[0], (1, 2, 0))
    eye = jnp.eye(N_HEADS, dtype=w_uv.dtype)
    w_uv_pad = jnp.einsum("lhv,hg->hlgv", w_uv[0], eye).reshape(N_HEADS, KV_LORA, N_HEADS * V_HEAD)

    bf = lambda a: a.astype(BF16)
    return {
        "w_in": bf(w_in_p), "g_q_a": g_q_a, "w_q": bf(w_q), "g_kv_a": g_kv_a,
        "w_k": bf(w_k), "w_v": bf(w_v), "w_uk_t": bf(w_uk_t), "w_uv_pad": bf(w_uv_pad),
        "w_dw": w_dw[0], "b_dw": b_dw, "g_cln": g_cln, "b_cln": b_cln, "w_out": bf(w_out[0]),
        "w_xq": bf(w_xq[0]), "w_xo": bf(w_xo[0]), "w_up": bf(w_up[0]), "w_fdw": w_fdw[0],
        "b_fdw": b_fdw, "w_down": bf(w_down[0]),
        "g_pre_mix": g_pre_mix, "g_post_mix": g_post_mix, "g_pre_x": g_pre_x, "g_post_x": g_post_x,
        "g_pre_ffn": g_pre_ffn, "g_post_ffn": g_post_ffn,
    }


TM_MIX = 512
TQ_FLASH = 512
FLASH_UNROLL = 8
T_POST = 512
T_FFN = 512
CONV_ROWS = 64
PIECES = 2
SAMPLE_SEQS = 16
SAMPLE_SEQS_POST = 8
PAGES_PER_CHUNK = 32
SUB_KEYS = 2048


def kernel(x_prompt, x_sample, cache_kv_latent, cache_k_rope, cache_mem_k, cache_mem_v, state_conv, state_ffn_conv, page_table, mem_prompt, w_in, g_q_a, w_q_b, g_kv_a, w_uk, w_uv, w_dw, b_dw, g_cln, b_cln, w_out, g_mem, w_xq, w_xk, w_xv, w_xo, w_up, w_fdw, b_fdw, w_down, g_pre_mix, g_post_mix, g_pre_x, g_post_x, g_pre_ffn, g_post_ffn):
    B, S, D = x_prompt.shape
    DB, DS, _ = x_sample.shape
    past_len = page_table.shape[1] * PAGE_SIZE
    wts = _prepare_weights(w_in, g_q_a, w_q_b, g_kv_a, w_uk, w_uv, w_dw, b_dw, g_cln, b_cln, w_out,
                           w_xq, w_xo, w_up, w_fdw, b_fdw, w_down,
                           g_pre_mix, g_post_mix, g_pre_x, g_post_x, g_pre_ffn, g_post_ffn)

    mk_p, mv_p, mkb_p, mvb_p = _mem_kv(mem_prompt, g_mem, w_xk[0].astype(BF16), w_xv[0].astype(BF16))
    tables_p = _rope_tables(S, 0, S)
    q, k, v, c_p, pe_p, u_p = _mix_in(x_prompt, tables_p, lambda b, i: (i, 0), wts, tm=TM_MIX, absorb=False)
    attn_p = _flash(q, k, v, tq=TQ_FLASH)
    halo_blocks = T_POST // CONV_HALO
    x2_p, cstate_p = _post_mix(
        x_prompt, attn_p, u_p, u_p, lambda b, i: (b, jnp.maximum(i * halo_blocks - 1, 0), 0),
        mkb_p, mvb_p, wts, G=1, T=T_POST, fresh_sequence=True, conv_rows=CONV_ROWS, n_pieces=PIECES)
    y_p, fstate_p = _ffn(x2_p, jnp.zeros((B, FFN_CONV_WIDTH - 1, D_FF), F32), wts, G=1, T=T_FFN, n_pieces=PIECES)

    rows = SAMPLE_SEQS * DS
    tables_s = _rope_tables(rows, past_len, DS)
    q_s, qlat_s, c_s, pe_s, u_s = _mix_in(
        x_sample.reshape(1, DB * DS, D), tables_s, lambda b, i: (0, 0), wts, tm=rows, absorb=True)
    qlat_s = qlat_s.reshape(DB, DS, N_HEADS, KV_LORA).transpose(0, 2, 1, 3).reshape(DB, N_HEADS * DS, KV_LORA)
    qpe_s = q_s.reshape(DB, DS, N_HEADS, HEAD_BLOCK)[..., ROPE_LANE0:ROPE_LANE0 + QK_ROPE]
    qpe_s = qpe_s.transpose(0, 2, 1, 3).reshape(DB, N_HEADS * DS, QK_ROPE)
    c_s = c_s.reshape(DB, DS, KV_LORA)
    pe_s = pe_s.reshape(DB, DS, QK_ROPE)
    attn_s = _paged_attention(page_table, qlat_s, qpe_s, c_s, pe_s, wts["w_uv_pad"],
                              cache_kv_latent.reshape(cache_kv_latent.shape[1:]),
                              jnp.swapaxes(cache_k_rope.reshape(cache_k_rope.shape[1:]), 1, 2),
                              pages_per_chunk=PAGES_PER_CHUNK, sub_keys=SUB_KEYS)
    n_state = CONV_WIDTH - 1
    halo_s = jnp.pad(state_conv[0], ((0, 0), (CONV_HALO - n_state, 0), (0, 0)))
    x2_s, cstate_s = _post_mix(
        x_sample, attn_s, u_s.reshape(DB, DS, D_CONV), halo_s, lambda b, i: (b, 0, 0),
        cache_mem_k.reshape(cache_mem_k.shape[1:]), cache_mem_v.reshape(cache_mem_v.shape[1:]), wts,
        G=SAMPLE_SEQS_POST, T=DS, fresh_sequence=False, conv_rows=DS, n_pieces=1)
    y_s, fstate_s = _ffn(x2_s, state_ffn_conv[0], wts, G=SAMPLE_SEQS, T=DS, n_pieces=1)

    mem_shape = (1, B, mem_prompt.shape[1], X_HEADS, X_HEAD_DIM)
    return (y_p, y_s, c_p[None], pe_p[None], cstate_p[None], fstate_p[None],
            mk_p.reshape(mem_shape), mv_p.reshape(mem_shape),
            c_s[None], pe_s[None], cstate_s[None], fstate_s[None])
```
